```python
import jax, jax.numpy as jnp
from jax import lax
import numpy as np

D_MODEL = 1024
BATCH = 1
SEQ = 16384
DEPTH = 2
DEC_BATCH = 8
DEC_SEQ = 16
PAST_LEN = 4096

CHUNK = 64
N_A_LAYERS = DEPTH // 2
N_B_LAYERS = DEPTH - N_A_LAYERS
POOL_WINDOWS = (2, 4, 8, 16)
N_POOL_GROUPS = len(POOL_WINDOWS)
POOL_GROUP = D_MODEL // N_POOL_GROUPS
POOL_HIST = max(POOL_WINDOWS) - 1
N_HEADS = 16
HEAD_DIM = D_MODEL // N_HEADS
PAST_CHUNKS = 8
BAND_PAST = PAST_CHUNKS * CHUNK
BAND = BAND_PAST + CHUNK
REL_CLIP = 256
PEER_HEADS = 8
PEER_NKEYS = 128
PEER_N = PEER_NKEYS * PEER_NKEYS
PEER_DQ = 256
PEER_HALF = PEER_DQ // 2
PEER_TOPK = 16
PEER_BLOCK = 128
EPS = 1e-6

kernel_name = "yoco_pool_chunkattn_peer_stream_step"


def rms_norm(x, g):
    xf = x.astype(jnp.float32)
    y = xf * lax.rsqrt(jnp.mean(xf * xf, axis=-1, keepdims=True) + EPS)
    return (y * g.astype(jnp.float32)).astype(x.dtype)


def pool_mix(xn, hist, pos0, w, scale):
    bsz, t, _ = xn.shape
    ext = jnp.concatenate([hist.astype(xn.dtype), xn], axis=1)
    cs = jnp.cumsum(ext.astype(jnp.float32), axis=1)
    cs = jnp.pad(cs, ((0, 0), (1, 0), (0, 0)))
    end = cs[:, POOL_HIST + 1:]
    pos = pos0 + jnp.arange(t)
    means = []
    for g, wnd in enumerate(POOL_WINDOWS):
        sl = slice(g * POOL_GROUP, (g + 1) * POOL_GROUP)
        start = cs[:, POOL_HIST + 1 - wnd:POOL_HIST + 1 - wnd + t, sl]
        cnt = jnp.minimum(pos + 1, wnd).astype(jnp.float32)[None, :, None]
        means.append((end[..., sl] - start) / cnt)
    pooled = jnp.concatenate(means, axis=-1) - xn.astype(jnp.float32)
    pooled = pooled.reshape(bsz, t, N_POOL_GROUPS, POOL_GROUP)
    out = jnp.einsum("btgc,gcd->btgd", pooled, w.astype(jnp.float32)).reshape(bsz, t, D_MODEL)
    out = out * scale.astype(jnp.float32)
    return out.astype(xn.dtype), ext[:, t:]


def shared_kv(h, kv_norm, w_kv, k_norm):
    bsz, t, _ = h.shape
    kv = rms_norm(h, kv_norm) @ w_kv
    k, v = jnp.split(kv, 2, axis=-1)
    k = rms_norm(k.reshape(bsz, t, N_HEADS, HEAD_DIM), k_norm)
    return k, v.reshape(bsz, t, N_HEADS, HEAD_DIM)


def band_attend(q, k, v, rel, valid, bias_table):
    s = jnp.einsum("bqhd,bkhd->bhqk", q.astype(jnp.float32), k.astype(jnp.float32)) * (HEAD_DIM ** -0.5)
    idx = jnp.clip(rel, -REL_CLIP, REL_CLIP) + REL_CLIP
    s = s + bias_table.astype(jnp.float32)[:, idx][None]
    s = jnp.where(valid[None, None], s, -1e30)
    p = jax.nn.softmax(s, axis=-1)
    o = jnp.einsum("bhqk,bkhd->bqhd", p, v.astype(jnp.float32))
    return o.astype(q.dtype)


def prompt_band_attention(q, k, v, bias_table):
    bsz, t, h, d = q.shape
    n_chunks = t // CHUNK
    kp = jnp.pad(k, ((0, 0), (BAND_PAST, 0), (0, 0), (0, 0)))
    vp = jnp.pad(v, ((0, 0), (BAND_PAST, 0), (0, 0), (0, 0)))
    qc = q.reshape(bsz, n_chunks, CHUNK, h, d).transpose(1, 0, 2, 3, 4)
    q_off = jnp.arange(CHUNK)
    k_off = jnp.arange(BAND) - BAND_PAST
    rel = q_off[:, None] - k_off[None, :]

    def one_chunk(args):
        c, qb = args
        kb = lax.dynamic_slice_in_dim(kp, c * CHUNK, BAND, axis=1)
        vb = lax.dynamic_slice_in_dim(vp, c * CHUNK, BAND, axis=1)
        valid = jnp.broadcast_to((c * CHUNK + k_off >= 0)[None, :], (CHUNK, BAND))
        return band_attend(qb, kb, vb, rel, valid, bias_table)

    out = lax.map(one_chunk, (jnp.arange(n_chunks), qc))
    return out.transpose(1, 0, 2, 3, 4).reshape(bsz, t, h, d)


def sample_band_attention(q, k_new, v_new, cache_k, cache_v, pos0, bias_table):
    t = q.shape[1]
    rows = cache_k.shape[1]
    k = jnp.concatenate([cache_k.astype(k_new.dtype), k_new], axis=1)
    v = jnp.concatenate([cache_v.astype(v_new.dtype), v_new], axis=1)
    q_pos = pos0 + jnp.arange(t)
    k_pos = pos0 - rows + jnp.arange(rows + t)
    qc, kc = q_pos // CHUNK, k_pos // CHUNK
    valid = ((kc[None, :] >= qc[:, None] - PAST_CHUNKS) & (kc[None, :] <= qc[:, None])
             & (k_pos[None, :] >= 0))
    rel = q_pos[:, None] - k_pos[None, :]
    return band_attend(q, k, v, rel, valid, bias_table)


def peer_ffn(xn, wq, keys, u, v):
    bsz, t, _ = xn.shape
    n = bsz * t
    n_pad = (-n) % PEER_BLOCK
    flat = jnp.pad(xn.reshape(n, D_MODEL), ((0, n_pad), (0, 0)))
    blocks = flat.reshape(-1, PEER_BLOCK, D_MODEL)

    def one_block(xb):
        q = (xb @ wq).reshape(PEER_BLOCK, PEER_HEADS, 2, PEER_HALF).astype(jnp.float32)
        s = jnp.einsum("nhpc,hpkc->nhpk", q, keys.astype(jnp.float32))
        s1, i1 = lax.top_k(s[:, :, 0], PEER_TOPK)
        s2, i2 = lax.top_k(s[:, :, 1], PEER_TOPK)
        cand = (s1[..., :, None] + s2[..., None, :]).reshape(PEER_BLOCK, PEER_HEADS, PEER_TOPK * PEER_TOPK)
        cand_idx = (i1[..., :, None] * PEER_NKEYS + i2[..., None, :]).reshape(PEER_BLOCK, PEER_HEADS, PEER_TOPK * PEER_TOPK)
        top_s, sel = lax.top_k(cand, PEER_TOPK)
        expert = jnp.take_along_axis(cand_idx, sel, axis=-1)
        g = jax.nn.softmax(top_s, axis=-1)
        ue = u[expert].astype(jnp.float32)
        ve = v[expert].astype(jnp.float32)
        a = jax.nn.gelu(jnp.einsum("nd,nhkd->nhk", xb.astype(jnp.float32), ue), approximate=False)
        return jnp.einsum("nhk,nhkd->nd", g * a, ve).astype(xb.dtype)

    out = lax.map(one_block, blocks).reshape(-1, D_MODEL)[:n]
    return out.reshape(bsz, t, D_MODEL)


def run_trunk(x, pool_hist, cache_k, cache_v, pos0, norm_mix, norm_ffn, pool_w, pool_scale,
              kv_norm, w_kv, k_norm, w_q, q_norm, rel_bias, w_o, peer_wq, peer_keys, peer_u, peer_v):
    bsz, t, _ = x.shape
    new_hist = []
    k_sh, v_sh = None, None
    for layer in range(DEPTH):
        xn = rms_norm(x, norm_mix[layer])
        if layer < N_A_LAYERS:
            mix, h = pool_mix(xn, pool_hist[layer], pos0, pool_w[layer], pool_scale[layer])
            new_hist.append(h)
        else:
            b = layer - N_A_LAYERS
            if b == 0:
                k_sh, v_sh = shared_kv(x, kv_norm, w_kv, k_norm)
            q = rms_norm((xn @ w_q[b]).reshape(bsz, t, N_HEADS, HEAD_DIM), q_norm[b])
            if cache_k is None:
                o = prompt_band_attention(q, k_sh, v_sh, rel_bias[b])
            else:
                o = sample_band_attention(q, k_sh, v_sh, cache_k, cache_v, pos0, rel_bias[b])
            mix = o.reshape(bsz, t, N_HEADS * HEAD_DIM) @ w_o[b]
        x = x + mix
        x = x + peer_ffn(rms_norm(x, norm_ffn[layer]), peer_wq[layer], peer_keys[layer],
                         peer_u[layer], peer_v[layer])
    return x, jnp.stack(new_hist, axis=0), k_sh, v_sh


def setup_inputs(seed: int = 0) -> dict:
    key = jax.random.key(seed)
    ks = jax.random.split(key, 20)
    f32 = jnp.float32
    kv_rows = min(BAND_PAST, PAST_LEN)

    def nrm(k, shape, s):
        return s * jax.random.normal(k, shape, f32)

    return {
        "x_prompt": nrm(ks[0], (BATCH, SEQ, D_MODEL), 1.0),
        "x_sample": nrm(ks[1], (DEC_BATCH, DEC_SEQ, D_MODEL), 1.0),
        "state_pool": nrm(ks[2], (N_A_LAYERS, DEC_BATCH, POOL_HIST, D_MODEL), 1.0),
        "cache_k": nrm(ks[3], (DEC_BATCH, kv_rows, N_HEADS, HEAD_DIM), 1.0),
        "cache_v": nrm(ks[4], (DEC_BATCH, kv_rows, N_HEADS, HEAD_DIM), 1.0),
        "norm_mix": 1.0 + nrm(ks[5], (DEPTH, D_MODEL), 0.02),
        "norm_ffn": 1.0 + nrm(ks[6], (DEPTH, D_MODEL), 0.02),
        "pool_w": nrm(ks[7], (N_A_LAYERS, N_POOL_GROUPS, POOL_GROUP, POOL_GROUP), POOL_GROUP ** -0.5),
        "pool_scale": 0.5 + nrm(ks[8], (N_A_LAYERS, D_MODEL), 0.05),
        "kv_norm": 1.0 + nrm(ks[9], (D_MODEL,), 0.02),
        "w_kv": nrm(ks[10], (D_MODEL, 2 * N_HEADS * HEAD_DIM), D_MODEL ** -0.5),
        "k_norm": 1.0 + nrm(ks[11], (HEAD_DIM,), 0.02),
        "w_q": nrm(ks[12], (N_B_LAYERS, D_MODEL, N_HEADS * HEAD_DIM), D_MODEL ** -0.5),
        "q_norm": 1.0 + nrm(ks[13], (N_B_LAYERS, HEAD_DIM), 0.02),
        "rel_bias": nrm(ks[14], (N_B_LAYERS, N_HEADS, 2 * REL_CLIP + 1), 0.1),
        "w_o": nrm(ks[15], (N_B_LAYERS, N_HEADS * HEAD_DIM, D_MODEL), (N_HEADS * HEAD_DIM) ** -0.5),
        "peer_wq": nrm(ks[16], (DEPTH, D_MODEL, PEER_HEADS * PEER_DQ), D_MODEL ** -0.5),
        "peer_keys": nrm(ks[17], (DEPTH, PEER_HEADS, 2, PEER_NKEYS, PEER_HALF), PEER_HALF ** -0.5),
        "peer_u": nrm(ks[18], (DEPTH, PEER_N, D_MODEL), D_MODEL ** -0.5),
        "peer_v": nrm(ks[19], (DEPTH, PEER_N, D_MODEL), 0.25),
    }


def reference(x_prompt, x_sample, state_pool, cache_k, cache_v, norm_mix, norm_ffn, pool_w, pool_scale,
              kv_norm, w_kv, k_norm, w_q, q_norm, rel_bias, w_o, peer_wq, peer_keys, peer_u, peer_v):
    hist0 = jnp.zeros((N_A_LAYERS, x_prompt.shape[0], POOL_HIST, D_MODEL), x_prompt.dtype)
    y_prompt, pool_p, k_p, v_p = run_trunk(
        x_prompt, hist0, None, None, 0, norm_mix, norm_ffn, pool_w, pool_scale,
        kv_norm, w_kv, k_norm, w_q, q_norm, rel_bias, w_o, peer_wq, peer_keys, peer_u, peer_v)
    y_sample, pool_s, k_s, v_s = run_trunk(
        x_sample, state_pool, cache_k, cache_v, PAST_LEN, norm_mix, norm_ffn, pool_w, pool_scale,
        kv_norm, w_kv, k_norm, w_q, q_norm, rel_bias, w_o, peer_wq, peer_keys, peer_u, peer_v)
    keep = min(BAND_PAST, x_prompt.shape[1])
    return (y_prompt, y_sample, pool_p, pool_s, k_p[:, -keep:], v_p[:, -keep:], k_s, v_s)
```

```python
import functools

import numpy as np
import jax
import jax.numpy as jnp
from jax import lax
from jax.experimental import pallas as pl
from jax.experimental.pallas import tpu as pltpu

D_MODEL = 1024
CHUNK = 64
POOL_WINDOWS = (2, 4, 8, 16)
POOL_GROUP = D_MODEL // len(POOL_WINDOWS)
POOL_HIST = max(POOL_WINDOWS) - 1
HALO = POOL_HIST + 1
N_HEADS = 16
HEAD_DIM = 64
PAST_CHUNKS = 8
BAND_PAST = PAST_CHUNKS * CHUNK
BAND = BAND_PAST + CHUNK
REL_CLIP = 256
PEER_HEADS = 8
PEER_NKEYS = 128
PEER_N = PEER_NKEYS * PEER_NKEYS
PEER_HALF = 128
PEER_TOPK = 16
EPS = 1e-6
NOT_RANKED = 127.0
NEG_INF = float("-inf")
MASKED = -1e30

VMEM_LIMIT_BYTES = 56 * 1024 * 1024

F32 = jnp.float32
BF16 = jnp.bfloat16
NT_DIMS = (((1,), (1,)), ((), ()))


def _rms(x, g):
    return x * lax.rsqrt(jnp.mean(x * x, axis=-1, keepdims=True) + EPS) * g


def _params(*sem):
    return pltpu.CompilerParams(dimension_semantics=sem, vmem_limit_bytes=VMEM_LIMIT_BYTES)


def _pool_kernel(pos0, tt, n_t, x_ref, xp_ref, h_ref, g_ref, w_ref, sc_ref, o_ref, st_ref):
    i = pl.program_id(1)
    g = g_ref[...]
    x = x_ref[0]
    xn = _rms(x, g)
    halo = jnp.where(i == 0, h_ref[0], _rms(xp_ref[0], g))
    buf = jnp.concatenate([halo, xn], axis=0)
    pos = pos0 + i * tt + lax.broadcasted_iota(jnp.int32, (tt, 1), 0)
    outs = []
    for gi, wnd in enumerate(POOL_WINDOWS):
        sl = slice(gi * POOL_GROUP, (gi + 1) * POOL_GROUP)
        s = buf[:, sl]
        sh = 1
        while sh < wnd:
            s = s + pltpu.roll(s, sh, axis=0)
            sh *= 2
        cnt = jnp.minimum(pos + 1, wnd).astype(F32)
        pooled = s[HALO:] / cnt - xn[:, sl]
        outs.append(jnp.dot(pooled, w_ref[gi], precision=lax.Precision.HIGHEST,
                            preferred_element_type=F32))
    mix = jnp.concatenate(outs, axis=-1) * sc_ref[...]
    o_ref[0] = x + mix

    @pl.when(i == n_t - 1)
    def _():
        st_ref[0] = buf[tt:]


def _pool_mixer(x, hist, pos0, g, w, scale, tt):
    bsz, t, _ = x.shape
    n_t = t // tt
    hist16 = jnp.pad(hist, ((0, 0), (HALO - POOL_HIST, 0), (0, 0)))
    per_tt = tt // HALO
    out, st = pl.pallas_call(
        functools.partial(_pool_kernel, pos0, tt, n_t),
        grid=(bsz, n_t),
        in_specs=[
            pl.BlockSpec((1, tt, D_MODEL), lambda b, i: (b, i, 0)),
            pl.BlockSpec((1, HALO, D_MODEL), lambda b, i: (b, jnp.maximum(i * per_tt - 1, 0), 0)),
            pl.BlockSpec((1, HALO, D_MODEL), lambda b, i: (b, 0, 0)),
            pl.BlockSpec((1, D_MODEL), lambda b, i: (0, 0)),
            pl.BlockSpec((len(POOL_WINDOWS), POOL_GROUP, POOL_GROUP), lambda b, i: (0, 0, 0)),
            pl.BlockSpec((1, D_MODEL), lambda b, i: (0, 0)),
        ],
        out_specs=[
            pl.BlockSpec((1, tt, D_MODEL), lambda b, i: (b, i, 0)),
            pl.BlockSpec((1, HALO, D_MODEL), lambda b, i: (b, 0, 0)),
        ],
        out_shape=[
            jax.ShapeDtypeStruct((bsz, t, D_MODEL), F32),
            jax.ShapeDtypeStruct((bsz, HALO, D_MODEL), F32),
        ],
        compiler_params=_params("arbitrary", "arbitrary"),
        name="pool_mixer",
    )(x, x, hist16, g.reshape(1, D_MODEL), w, scale.reshape(1, D_MODEL))
    return out, st[:, HALO - POOL_HIST:]


def _top_ranks(s):
    rank = jnp.full(s.shape, NOT_RANKED, F32)
    vals = []
    for k in range(PEER_TOPK):
        m = jnp.max(s, axis=0, keepdims=True)
        hit = s == m
        rank = jnp.where(hit, float(k), rank)
        s = jnp.where(hit, NEG_INF, s)
        vals.append(m)
    return vals, rank


def _rows_to_block(rows):
    n = len(rows)
    rid = lax.broadcasted_iota(jnp.int32, (n, rows[0].shape[1]), 0)
    blk = jnp.zeros((n, rows[0].shape[1]), F32)
    for k, r in enumerate(rows):
        blk = jnp.where(rid == k, r, blk)
    return blk


def _select_kernel(x_ref, g_ref, wq_ref, keys_ref, xn_ref, r2_ref, l_ref, c_ref, e2_ref):
    h = pl.program_id(1)

    @pl.when(h == 0)
    def _():
        xn_ref[...] = _rms(x_ref[...], g_ref[...]).astype(BF16)

    q_t = lax.dot_general(wq_ref[...], xn_ref[...], NT_DIMS, preferred_element_type=F32)
    s1 = jnp.dot(keys_ref[0, 0], q_t[:PEER_HALF], precision=lax.Precision.HIGHEST,
                 preferred_element_type=F32)
    s2 = jnp.dot(keys_ref[0, 1], q_t[PEER_HALF:], precision=lax.Precision.HIGHEST,
                 preferred_element_type=F32)
    v1, rank1 = _top_ranks(s1)
    v2, rank2 = _top_ranks(s2)
    v2_blk = _rows_to_block(v2)
    cands = [v1[r1] + v2_blk for r1 in range(PEER_TOPK)]
    x = jnp.concatenate(cands, axis=0)
    tops = []
    for _ in range(PEER_TOPK):
        m = jnp.max(x, axis=0, keepdims=True)
        x = jnp.where(x == m, NEG_INF, x)
        tops.append(m)
    tau = tops[-1]
    z = jnp.zeros_like(tau)
    for m in tops:
        z = z + jnp.exp(m - tops[0])
    l_of_i1 = jnp.zeros(s1.shape, F32)
    for r1 in range(PEER_TOPK):
        n_sel = jnp.sum((cands[r1] >= tau).astype(F32), axis=0, keepdims=True)
        l_of_i1 = jnp.where(rank1 == float(r1), n_sel, l_of_i1)
    r2_ref[0] = rank2
    l_ref[0] = l_of_i1
    c_ref[0] = jnp.exp(s1 - v1[0]) / z
    e2_ref[0] = jnp.exp(s2 - v2[0])


def _peer_select(x, g, wq_t, keys, tb):
    t = x.shape[0]
    feat = jax.ShapeDtypeStruct((PEER_HEADS, PEER_NKEYS, t), F32)
    feat_spec = pl.BlockSpec((1, PEER_NKEYS, tb), lambda i, h: (h, 0, i))
    return pl.pallas_call(
        _select_kernel,
        grid=(t // tb, PEER_HEADS),
        in_specs=[
            pl.BlockSpec((tb, D_MODEL), lambda i, h: (i, 0)),
            pl.BlockSpec((1, D_MODEL), lambda i, h: (0, 0)),
            pl.BlockSpec((2 * PEER_HALF, D_MODEL), lambda i, h: (h, 0)),
            pl.BlockSpec((1, 2, PEER_NKEYS, PEER_HALF), lambda i, h: (h, 0, 0, 0)),
        ],
        out_specs=[pl.BlockSpec((tb, D_MODEL), lambda i, h: (i, 0)),
                   feat_spec, feat_spec, feat_spec, feat_spec],
        out_shape=[jax.ShapeDtypeStruct((t, D_MODEL), BF16), feat, feat, feat, feat],
        compiler_params=_params("arbitrary", "arbitrary"),
        name="peer_select",
    )(x, g.reshape(1, D_MODEL), wq_t, keys)


I1_PER_CHUNK = 8
E_CHUNK = I1_PER_CHUNK * PEER_NKEYS


def _dense_kernel(n_chunks, x_ref, xn_ref, u_ref, vt_ref, r2_ref, e2_ref, l_ref, c_ref, o_ref, acc_ref):
    c = pl.program_id(1)

    @pl.when(c == 0)
    def _():
        acc_ref[...] = jnp.zeros_like(acc_ref)

    a_t = lax.dot_general(u_ref[...], xn_ref[...], NT_DIMS, preferred_element_type=F32)
    pieces = []
    for j in range(I1_PER_CHUNK):
        gate = jnp.zeros((PEER_NKEYS, a_t.shape[1]), F32)
        for h in range(PEER_HEADS):
            l_row = l_ref[h, j:j + 1, :]
            c_row = c_ref[h, j:j + 1, :]
            gate = gate + jnp.where(r2_ref[h] < l_row, e2_ref[h] * c_row, 0.0)
        a_j = a_t[j * PEER_NKEYS:(j + 1) * PEER_NKEYS]
        act = 0.5 * a_j * (1.0 + lax.erf(a_j * (2.0 ** -0.5)))
        pieces.append((act * gate).astype(BF16))
    h_t = jnp.concatenate(pieces, axis=0)
    acc_ref[...] += jnp.dot(vt_ref[...], h_t, preferred_element_type=F32)

    @pl.when(c == n_chunks - 1)
    def _():
        o_ref[...] = x_ref[...] + acc_ref[...].T


def _peer_dense(x, xn, u, v_t, r2, e2, l, cc, tb):
    t = x.shape[0]
    n_chunks = PEER_N // E_CHUNK
    tok = pl.BlockSpec((tb, D_MODEL), lambda i, c: (i, 0))
    full = pl.BlockSpec((PEER_HEADS, PEER_NKEYS, tb), lambda i, c: (0, 0, i))
    part = pl.BlockSpec((PEER_HEADS, I1_PER_CHUNK, tb), lambda i, c: (0, c, i))
    return pl.pallas_call(
        functools.partial(_dense_kernel, n_chunks),
        grid=(t // tb, n_chunks),
        in_specs=[tok, tok,
                  pl.BlockSpec((E_CHUNK, D_MODEL), lambda i, c: (c, 0)),
                  pl.BlockSpec((D_MODEL, E_CHUNK), lambda i, c: (0, c)),
                  full, full, part, part],
        out_specs=tok,
        out_shape=jax.ShapeDtypeStruct((t, D_MODEL), F32),
        scratch_shapes=[pltpu.VMEM((D_MODEL, tb), F32)],
        compiler_params=_params("arbitrary", "arbitrary"),
        name="peer_dense",
    )(x, xn, u, v_t, r2, e2, l, cc)


def _peer(x, g, wq_t, keys, u, v_t, tb_sel, tb_dense):
    xn, r2, l, cc, e2 = _peer_select(x, g, wq_t, keys, tb_sel)
    return _peer_dense(x, xn, u, v_t, r2, e2, l, cc, tb_dense)


def _head_norm(z, head_mean, gn):
    zz = z * z
    hi = zz.astype(BF16)
    lo = (zz - hi.astype(F32)).astype(BF16)
    ms = (jnp.dot(hi, head_mean, preferred_element_type=F32)
          + jnp.dot(lo, head_mean, preferred_element_type=F32))
    return z * lax.rsqrt(ms + EPS) * gn


def _qkv_kernel(x_ref, gq_ref, gkv_ref, wq_ref, wkv_ref, hm_ref, qn_ref, kn_ref, q_ref, k_ref, v_ref):
    x = x_ref[...]
    hm = hm_ref[...]
    q = jnp.dot(_rms(x, gq_ref[...]).astype(BF16), wq_ref[...], preferred_element_type=F32)
    kv = jnp.dot(_rms(x, gkv_ref[...]).astype(BF16), wkv_ref[...], preferred_element_type=F32)
    q_ref[...] = _head_norm(q, hm, qn_ref[...])
    k_ref[...] = _head_norm(kv[:, :D_MODEL], hm, kn_ref[...])
    v_ref[...] = kv[:, D_MODEL:]


def _qkv(x, gq, gkv, wq, wkv, qn, kn, tb):
    t = x.shape[0]
    head_id = np.arange(D_MODEL) // HEAD_DIM
    head_mean = jnp.asarray((head_id[:, None] == head_id[None, :]) / HEAD_DIM, BF16)
    tok = pl.BlockSpec((tb, D_MODEL), lambda i: (i, 0))
    vec = pl.BlockSpec((1, D_MODEL), lambda i: (0, 0))
    sq = pl.BlockSpec((D_MODEL, D_MODEL), lambda i: (0, 0))
    out = jax.ShapeDtypeStruct((t, D_MODEL), F32)
    return pl.pallas_call(
        _qkv_kernel,
        grid=(t // tb,),
        in_specs=[tok, vec, vec, sq, pl.BlockSpec((D_MODEL, 2 * D_MODEL), lambda i: (0, 0)), sq, vec, vec],
        out_specs=[tok, tok, tok],
        out_shape=[out, out, out],
        compiler_params=_params("arbitrary"),
        name="qkv_proj",
    )(x, gq.reshape(1, D_MODEL), gkv.reshape(1, D_MODEL), wq, wkv, head_mean,
      jnp.tile(qn, N_HEADS).reshape(1, D_MODEL), jnp.tile(kn, N_HEADS).reshape(1, D_MODEL))


def _attend_rows(q_rows, k_band, v_band, bias_ref, valid):
    lane = lax.broadcasted_iota(jnp.int32, (1, 2 * HEAD_DIM), 1)
    first = lane < HEAD_DIM
    outs = []
    for hp in range(N_HEADS // 2):
        sl = slice(hp * 2 * HEAD_DIM, (hp + 1) * 2 * HEAD_DIM)
        q2, k2, v2 = q_rows[:, sl], k_band[:, sl], v_band[:, sl]
        o_pair = []
        for e in range(2):
            qm = jnp.where(first if e == 0 else jnp.logical_not(first), q2, 0.0).astype(BF16)
            s = lax.dot_general(qm, k2, NT_DIMS, preferred_element_type=F32)
            s = s * (HEAD_DIM ** -0.5) + bias_ref[2 * hp + e]
            if valid is not None:
                s = jnp.where(valid, s, MASKED)
            p = jnp.exp(s - jnp.max(s, axis=-1, keepdims=True))
            den = jnp.sum(p, axis=-1, keepdims=True)
            o_pair.append(jnp.dot(p.astype(BF16), v2, preferred_element_type=F32) / den)
        outs.append(jnp.where(first, o_pair[0], o_pair[1]))
    return jnp.concatenate(outs, axis=-1)


def _attn_prompt_kernel(qb, q_ref, kp_ref, kc_ref, vp_ref, vc_ref, bias_ref, wo_ref, x_ref, o_ref,
                        kcat, vcat, oscr):
    i = pl.program_id(0)
    kcat[:qb] = kp_ref[...].astype(BF16)
    kcat[qb:] = kc_ref[...].astype(BF16)
    vcat[:qb] = vp_ref[...].astype(BF16)
    vcat[qb:] = vc_ref[...].astype(BF16)
    col = lax.broadcasted_iota(jnp.int32, (1, BAND), 1)

    def chunk(j, carry):
        r0 = pl.multiple_of(j * CHUNK, CHUNK)
        k0 = pl.multiple_of(qb - BAND_PAST + j * CHUNK, CHUNK)
        valid = (i * qb + j * CHUNK - BAND_PAST + col) >= 0
        o = _attend_rows(q_ref[pl.ds(r0, CHUNK), :], kcat[pl.ds(k0, BAND), :], vcat[pl.ds(k0, BAND), :],
                         bias_ref, valid)
        oscr[pl.ds(r0, CHUNK), :] = o.astype(BF16)
        return carry

    lax.fori_loop(0, qb // CHUNK, chunk, 0)
    o_ref[...] = x_ref[...] + jnp.dot(oscr[...], wo_ref[...], preferred_element_type=F32)


def _attn_prompt(x, q, k, v, bias, wo, qb):
    t = x.shape[0]
    cur = pl.BlockSpec((qb, D_MODEL), lambda i: (i, 0))
    prev = pl.BlockSpec((qb, D_MODEL), lambda i: (jnp.maximum(i - 1, 0), 0))
    return pl.pallas_call(
        functools.partial(_attn_prompt_kernel, qb),
        grid=(t // qb,),
        in_specs=[cur, prev, cur, prev, cur,
                  pl.BlockSpec((N_HEADS, CHUNK, BAND), lambda i: (0, 0, 0)),
                  pl.BlockSpec((D_MODEL, D_MODEL), lambda i: (0, 0)),
                  cur],
        out_specs=cur,
        out_shape=jax.ShapeDtypeStruct((t, D_MODEL), F32),
        scratch_shapes=[pltpu.VMEM((2 * qb, D_MODEL), BF16), pltpu.VMEM((2 * qb, D_MODEL), BF16),
                        pltpu.VMEM((qb, D_MODEL), BF16)],
        compiler_params=_params("arbitrary"),
        name="attn_prompt",
    )(q, k, k, v, v, bias, wo, x)


def _attn_sample_kernel(rows, q_ref, ck_ref, kn_ref, cv_ref, vn_ref, bias_ref, wo_ref, x_ref, o_ref,
                        kcat, vcat):
    kcat[:rows] = ck_ref[0].astype(BF16)
    kcat[rows:] = kn_ref[0].astype(BF16)
    vcat[:rows] = cv_ref[0].astype(BF16)
    vcat[rows:] = vn_ref[0].astype(BF16)
    o = _attend_rows(q_ref[0], kcat[...], vcat[...], bias_ref, None)
    o_ref[0] = x_ref[0] + jnp.dot(o.astype(BF16), wo_ref[...], preferred_element_type=F32)


def _attn_sample(x, q, k, v, cache_k, cache_v, bias, wo):
    bsz, t, _ = x.shape
    rows = cache_k.shape[1]
    new = pl.BlockSpec((1, t, D_MODEL), lambda b: (b, 0, 0))
    old = pl.BlockSpec((1, rows, D_MODEL), lambda b: (b, 0, 0))
    return pl.pallas_call(
        functools.partial(_attn_sample_kernel, rows),
        grid=(bsz,),
        in_specs=[new, old, new, old, new,
                  pl.BlockSpec((N_HEADS, t, rows + t), lambda b: (0, 0, 0)),
                  pl.BlockSpec((D_MODEL, D_MODEL), lambda b: (0, 0)),
                  new],
        out_specs=new,
        out_shape=jax.ShapeDtypeStruct((bsz, t, D_MODEL), F32),
        scratch_shapes=[pltpu.VMEM((rows + t, D_MODEL), BF16), pltpu.VMEM((rows + t, D_MODEL), BF16)],
        compiler_params=_params("arbitrary"),
        name="attn_sample",
    )(q, cache_k, k, cache_v, v, bias, wo, x)


def _band_bias(rel_bias, n_q, n_k, valid=None):
    rel = np.arange(n_q)[:, None] - np.arange(n_k)[None, :] + BAND_PAST
    idx = np.clip(rel, -REL_CLIP, REL_CLIP) + REL_CLIP
    bias = rel_bias[:, idx]
    if valid is not None:
        bias = jnp.where(valid[None], bias, MASKED)
    return bias


def _trunk(x, hist, cache_k, cache_v, pos0, p, tt, tb_sel, tb_dense, tb_qkv, qb):
    bsz, t, _ = x.shape
    n = bsz * t
    x1, st = _pool_mixer(x, hist, pos0, p["norm_mix"][0], p["pool_w"][0], p["pool_scale"][0], tt)
    x1 = x1.reshape(n, D_MODEL)
    x2 = _peer(x1, p["norm_ffn"][0], p["wq_t"][0], p["peer_keys"][0], p["u"][0], p["v_t"][0],
               tb_sel, tb_dense)
    q, k, v = _qkv(x2, p["norm_mix"][1], p["kv_norm"], p["w_q"], p["w_kv"], p["q_norm"], p["k_norm"], tb_qkv)
    if cache_k is None:
        assert bsz == 1 and t % qb == 0 and qb >= BAND_PAST and pos0 == 0
        x3 = _attn_prompt(x2, q, k, v, _band_bias(p["rel_bias"], CHUNK, BAND), p["w_o"], qb)
    else:
        rows = cache_k.shape[1]
        assert pos0 % CHUNK == 0 and t <= CHUNK and rows == BAND_PAST and pos0 >= rows
        bias = _band_bias(p["rel_bias"], t, rows + t)
        x3 = _attn_sample(x2.reshape(bsz, t, D_MODEL), q.reshape(bsz, t, D_MODEL),
                          k.reshape(bsz, t, D_MODEL), v.reshape(bsz, t, D_MODEL),
                          cache_k.reshape(bsz, rows, D_MODEL), cache_v.reshape(bsz, rows, D_MODEL),
                          bias, p["w_o"]).reshape(n, D_MODEL)
    x4 = _peer(x3, p["norm_ffn"][1], p["wq_t"][1], p["peer_keys"][1], p["u"][1], p["v_t"][1],
               tb_sel, tb_dense)
    return (x4.reshape(bsz, t, D_MODEL), st[None],
            k.reshape(bsz, t, N_HEADS, HEAD_DIM), v.reshape(bsz, t, N_HEADS, HEAD_DIM))


def _prepare(norm_mix, norm_ffn, pool_w, pool_scale, kv_norm, w_kv, k_norm, w_q, q_norm, rel_bias, w_o,
             peer_wq, peer_keys, peer_u, peer_v):
    assert w_q.shape[0] == 1 and pool_w.shape[0] == 1
    return dict(
        norm_mix=norm_mix, norm_ffn=norm_ffn, pool_w=pool_w, pool_scale=pool_scale, kv_norm=kv_norm,
        k_norm=k_norm, q_norm=q_norm[0], rel_bias=rel_bias[0], peer_keys=peer_keys,
        w_kv=w_kv.astype(BF16), w_q=w_q[0].astype(BF16), w_o=w_o[0].astype(BF16),
        wq_t=jnp.swapaxes(peer_wq, 1, 2).astype(BF16),
        u=peer_u.astype(BF16), v_t=jnp.swapaxes(peer_v, 1, 2).astype(BF16))


def kernel(x_prompt, x_sample, state_pool, cache_k, cache_v, norm_mix, norm_ffn, pool_w, pool_scale, kv_norm, w_kv, k_norm, w_q, q_norm, rel_bias, w_o, peer_wq, peer_keys, peer_u, peer_v):
    p = _prepare(norm_mix, norm_ffn, pool_w, pool_scale, kv_norm, w_kv, k_norm, w_q, q_norm, rel_bias, w_o,
                 peer_wq, peer_keys, peer_u, peer_v)
    hist0 = jnp.zeros((x_prompt.shape[0], POOL_HIST, D_MODEL), x_prompt.dtype)
    y_p, pool_p, k_p, v_p = _trunk(x_prompt, hist0, None, None, 0, p,
                                   tt=512, tb_sel=256, tb_dense=512, tb_qkv=512, qb=512)
    past_len = 4096
    y_s, pool_s, k_s, v_s = _trunk(x_sample, state_pool[0], cache_k, cache_v, past_len, p,
                                   tt=x_sample.shape[1], tb_sel=128, tb_dense=128, tb_qkv=128, qb=None)
    keep = min(BAND_PAST, x_prompt.shape[1])
    return (y_p, y_s, pool_p, pool_s, k_p[:, -keep:], v_p[:, -keep:], k_s, v_s)
```

```python
import functools

import numpy as np
import jax
import jax.numpy as jnp
from jax import lax
from jax.experimental import pallas as pl
from jax.experimental.pallas import tpu as pltpu

D_MODEL = 1024
CHUNK = 64
POOL_WINDOWS = (2, 4, 8, 16)
POOL_GROUP = D_MODEL // len(POOL_WINDOWS)
POOL_HIST = max(POOL_WINDOWS) - 1
HALO = POOL_HIST + 1
N_HEADS = 16
HEAD_DIM = 64
PAST_CHUNKS = 8
BAND_PAST = PAST_CHUNKS * CHUNK
BAND = BAND_PAST + CHUNK
REL_CLIP = 256
PEER_HEADS = 8
PEER_NKEYS = 128
PEER_N = PEER_NKEYS * PEER_NKEYS
PEER_HALF = 128
PEER_TOPK = 16
EPS = 1e-6
NOT_RANKED = 127.0
NEG_INF = float("-inf")
MASKED = -1e30

VMEM_LIMIT_BYTES = 56 * 1024 * 1024

F32 = jnp.float32
BF16 = jnp.bfloat16
NT_DIMS = (((1,), (1,)), ((), ()))


def _rms(x, g):
    return x * lax.rsqrt(jnp.mean(x * x, axis=-1, keepdims=True) + EPS) * g


def _params(*sem):
    return pltpu.CompilerParams(dimension_semantics=sem, vmem_limit_bytes=VMEM_LIMIT_BYTES)


def _pool_kernel(pos0, tt, n_t, x_ref, xp_ref, h_ref, g_ref, w_ref, sc_ref, o_ref, st_ref):
    i = pl.program_id(1)
    g = g_ref[...]
    x = x_ref[0]
    xn = _rms(x, g)
    halo = jnp.where(i == 0, h_ref[0], _rms(xp_ref[0], g))
    buf = jnp.concatenate([halo, xn], axis=0)
    pos = pos0 + i * tt + lax.broadcasted_iota(jnp.int32, (tt, 1), 0)
    outs = []
    for gi, wnd in enumerate(POOL_WINDOWS):
        sl = slice(gi * POOL_GROUP, (gi + 1) * POOL_GROUP)
        s = buf[:, sl]
        sh = 1
        while sh < wnd:
            s = s + pltpu.roll(s, sh, axis=0)
            sh *= 2
        cnt = jnp.minimum(pos + 1, wnd).astype(F32)
        pooled = s[HALO:] / cnt - xn[:, sl]
        outs.append(jnp.dot(pooled, w_ref[gi], precision=lax.Precision.HIGHEST,
                            preferred_element_type=F32))
    mix = jnp.concatenate(outs, axis=-1) * sc_ref[...]
    o_ref[0] = x + mix

    @pl.when(i == n_t - 1)
    def _():
        st_ref[0] = buf[tt:]


def _pool_mixer(x, hist, pos0, g, w, scale, tt):
    bsz, t, _ = x.shape
    n_t = t // tt
    hist16 = jnp.pad(hist, ((0, 0), (HALO - POOL_HIST, 0), (0, 0)))
    per_tt = tt // HALO
    out, st = pl.pallas_call(
        functools.partial(_pool_kernel, pos0, tt, n_t),
        grid=(bsz, n_t),
        in_specs=[
            pl.BlockSpec((1, tt, D_MODEL), lambda b, i: (b, i, 0)),
            pl.BlockSpec((1, HALO, D_MODEL), lambda b, i: (b, jnp.maximum(i * per_tt - 1, 0), 0)),
            pl.BlockSpec((1, HALO, D_MODEL), lambda b, i: (b, 0, 0)),
            pl.BlockSpec((1, D_MODEL), lambda b, i: (0, 0)),
            pl.BlockSpec((len(POOL_WINDOWS), POOL_GROUP, POOL_GROUP), lambda b, i: (0, 0, 0)),
            pl.BlockSpec((1, D_MODEL), lambda b, i: (0, 0)),
        ],
        out_specs=[
            pl.BlockSpec((1, tt, D_MODEL), lambda b, i: (b, i, 0)),
            pl.BlockSpec((1, HALO, D_MODEL), lambda b, i: (b, 0, 0)),
        ],
        out_shape=[
            jax.ShapeDtypeStruct((bsz, t, D_MODEL), F32),
            jax.ShapeDtypeStruct((bsz, HALO, D_MODEL), F32),
        ],
        compiler_params=_params("arbitrary", "arbitrary"),
        name="pool_mixer",
    )(x, x, hist16, g.reshape(1, D_MODEL), w, scale.reshape(1, D_MODEL))
    return out, st[:, HALO - POOL_HIST:]


def _top_ranks(s):
    rank = jnp.full(s.shape, NOT_RANKED, F32)
    vals = []
    for k in range(PEER_TOPK):
        m = jnp.max(s, axis=0, keepdims=True)
        hit = s == m
        rank = jnp.where(hit, float(k), rank)
        s = jnp.where(hit, NEG_INF, s)
        vals.append(m)
    return vals, rank


def _rows_to_block(rows):
    n = len(rows)
    rid = lax.broadcasted_iota(jnp.int32, (n, rows[0].shape[1]), 0)
    blk = jnp.zeros((n, rows[0].shape[1]), F32)
    for k, r in enumerate(rows):
        blk = jnp.where(rid == k, r, blk)
    return blk


def _bf16_pair(x):
    hi = pltpu.bitcast(x.astype(BF16).astype(F32), jnp.uint32)
    return hi | (hi >> 16)


def _select_kernel(x_ref, g_ref, wq_ref, keys_ref, xn_ref, r2_ref, l_ref, c_ref, e2_ref):
    h = pl.program_id(1)

    @pl.when(h == 0)
    def _():
        xn_ref[...] = _rms(x_ref[...], g_ref[...]).astype(BF16)

    q_t = lax.dot_general(wq_ref[...], xn_ref[...], NT_DIMS, preferred_element_type=F32)
    s1 = jnp.dot(keys_ref[0, 0], q_t[:PEER_HALF], precision=lax.Precision.HIGHEST,
                 preferred_element_type=F32)
    s2 = jnp.dot(keys_ref[0, 1], q_t[PEER_HALF:], precision=lax.Precision.HIGHEST,
                 preferred_element_type=F32)
    v1, rank1 = _top_ranks(s1)
    v2, rank2 = _top_ranks(s2)
    v2_blk = _rows_to_block(v2)
    cands = [v1[r1] + v2_blk for r1 in range(PEER_TOPK)]
    x = jnp.concatenate(cands, axis=0)
    tops = []
    for _ in range(PEER_TOPK):
        m = jnp.max(x, axis=0, keepdims=True)
        x = jnp.where(x == m, NEG_INF, x)
        tops.append(m)
    tau = tops[-1]
    z = jnp.zeros_like(tau)
    for m in tops:
        z = z + jnp.exp(m - tops[0])
    l_of_i1 = jnp.zeros(s1.shape, F32)
    for r1 in range(PEER_TOPK):
        n_sel = jnp.sum((cands[r1] >= tau).astype(F32), axis=0, keepdims=True)
        l_of_i1 = jnp.where(rank1 == float(r1), n_sel, l_of_i1)
    r2_ref[0] = rank2.astype(BF16)
    l_ref[0] = _bf16_pair(l_of_i1)
    c_ref[0] = _bf16_pair(jnp.exp(s1 - v1[0]) / z)
    e2_ref[0] = jnp.exp(s2 - v2[0]).astype(BF16)


def _peer_select(x, g, wq_t, keys, tb):
    t = x.shape[0]
    pair = jax.ShapeDtypeStruct((PEER_HEADS, PEER_NKEYS, t), jnp.uint32)
    feat16 = jax.ShapeDtypeStruct((PEER_HEADS, PEER_NKEYS, t), BF16)
    feat_spec = pl.BlockSpec((1, PEER_NKEYS, tb), lambda i, h: (h, 0, i))
    return pl.pallas_call(
        _select_kernel,
        grid=(t // tb, PEER_HEADS),
        in_specs=[
            pl.BlockSpec((tb, D_MODEL), lambda i, h: (i, 0)),
            pl.BlockSpec((1, D_MODEL), lambda i, h: (0, 0)),
            pl.BlockSpec((2 * PEER_HALF, D_MODEL), lambda i, h: (h, 0)),
            pl.BlockSpec((1, 2, PEER_NKEYS, PEER_HALF), lambda i, h: (h, 0, 0, 0)),
        ],
        out_specs=[pl.BlockSpec((tb, D_MODEL), lambda i, h: (i, 0)),
                   feat_spec, feat_spec, feat_spec, feat_spec],
        out_shape=[jax.ShapeDtypeStruct((t, D_MODEL), BF16), feat16, pair, pair, feat16],
        compiler_params=_params("arbitrary", "arbitrary"),
        name="peer_select",
    )(x, g.reshape(1, D_MODEL), wq_t, keys)


I1_PER_CHUNK = 16
E_CHUNK = I1_PER_CHUNK * PEER_NKEYS


BF16_ROWS = 16
F32_ROWS = 8


def _packed_rows(word_row):
    return pltpu.bitcast(jnp.broadcast_to(word_row, (F32_ROWS, word_row.shape[1])), BF16)


def _dense_kernel(n_chunks, x_ref, xn_ref, u_ref, vt_ref, r2_ref, e2_ref, l_ref, c_ref, o_ref, acc_ref):
    c = pl.program_id(1)

    @pl.when(c == 0)
    def _():
        acc_ref[...] = jnp.zeros_like(acc_ref)

    a_t = lax.dot_general(u_ref[...], xn_ref[...], NT_DIMS, preferred_element_type=F32)
    pieces = []
    for j in range(I1_PER_CHUNK):
        rows = [(_packed_rows(l_ref[h, j:j + 1, :]), _packed_rows(c_ref[h, j:j + 1, :]))
                for h in range(PEER_HEADS)]
        for b in range(PEER_NKEYS // BF16_ROWS):
            i2 = slice(b * BF16_ROWS, (b + 1) * BF16_ROWS)
            gate = None
            for h in range(PEER_HEADS):
                l_rows, c_rows = rows[h]
                term = jnp.where(r2_ref[h, i2, :] < l_rows, e2_ref[h, i2, :] * c_rows, jnp.zeros((), BF16))
                gate = term if gate is None else gate + term
            r0 = j * PEER_NKEYS + b * BF16_ROWS
            a_b = a_t[r0:r0 + BF16_ROWS].astype(BF16)
            act = 0.5 * a_b * (1.0 + lax.erf(a_b * (2.0 ** -0.5)))
            pieces.append(act * gate)
    h_t = jnp.concatenate(pieces, axis=0)
    acc_ref[...] += jnp.dot(vt_ref[...], h_t, preferred_element_type=F32)

    @pl.when(c == n_chunks - 1)
    def _():
        o_ref[...] = x_ref[...] + acc_ref[...].T


def _peer_dense(x, xn, u, v_t, r2, e2, l, cc, tb):
    t = x.shape[0]
    n_chunks = PEER_N // E_CHUNK
    tok = pl.BlockSpec((tb, D_MODEL), lambda i, c: (i, 0))
    full = pl.BlockSpec((PEER_HEADS, PEER_NKEYS, tb), lambda i, c: (0, 0, i))
    part = pl.BlockSpec((PEER_HEADS, I1_PER_CHUNK, tb), lambda i, c: (0, c, i))
    return pl.pallas_call(
        functools.partial(_dense_kernel, n_chunks),
        grid=(t // tb, n_chunks),
        in_specs=[tok, tok,
                  pl.BlockSpec((E_CHUNK, D_MODEL), lambda i, c: (c, 0)),
                  pl.BlockSpec((D_MODEL, E_CHUNK), lambda i, c: (0, c)),
                  full, full, part, part],
        out_specs=tok,
        out_shape=jax.ShapeDtypeStruct((t, D_MODEL), F32),
        scratch_shapes=[pltpu.VMEM((D_MODEL, tb), F32)],
        compiler_params=_params("arbitrary", "arbitrary"),
        name="peer_dense",
    )(x, xn, u, v_t, r2, e2, l, cc)


def _peer(x, g, wq_t, keys, u, v_t, tb_sel, tb_dense):
    xn, r2, l, cc, e2 = _peer_select(x, g, wq_t, keys, tb_sel)
    return _peer_dense(x, xn, u, v_t, r2, e2, l, cc, tb_dense)


def _head_norm(z, head_mean, gn):
    zz = z * z
    hi = zz.astype(BF16)
    lo = (zz - hi.astype(F32)).astype(BF16)
    ms = (jnp.dot(hi, head_mean, preferred_element_type=F32)
          + jnp.dot(lo, head_mean, preferred_element_type=F32))
    return z * lax.rsqrt(ms + EPS) * gn


def _qkv_kernel(x_ref, gq_ref, gkv_ref, wq_ref, wkv_ref, hm_ref, qn_ref, kn_ref, q_ref, k_ref, v_ref):
    x = x_ref[...]
    hm = hm_ref[...]
    q = jnp.dot(_rms(x, gq_ref[...]).astype(BF16), wq_ref[...], preferred_element_type=F32)
    kv = jnp.dot(_rms(x, gkv_ref[...]).astype(BF16), wkv_ref[...], preferred_element_type=F32)
    q_ref[...] = _head_norm(q, hm, qn_ref[...])
    k_ref[...] = _head_norm(kv[:, :D_MODEL], hm, kn_ref[...])
    v_ref[...] = kv[:, D_MODEL:]


def _qkv(x, gq, gkv, wq, wkv, qn, kn, tb):
    t = x.shape[0]
    head_id = np.arange(D_MODEL) // HEAD_DIM
    head_mean = jnp.asarray((head_id[:, None] == head_id[None, :]) / HEAD_DIM, BF16)
    tok = pl.BlockSpec((tb, D_MODEL), lambda i: (i, 0))
    vec = pl.BlockSpec((1, D_MODEL), lambda i: (0, 0))
    sq = pl.BlockSpec((D_MODEL, D_MODEL), lambda i: (0, 0))
    out = jax.ShapeDtypeStruct((t, D_MODEL), F32)
    return pl.pallas_call(
        _qkv_kernel,
        grid=(t // tb,),
        in_specs=[tok, vec, vec, sq, pl.BlockSpec((D_MODEL, 2 * D_MODEL), lambda i: (0, 0)), sq, vec, vec],
        out_specs=[tok, tok, tok],
        out_shape=[out, out, out],
        compiler_params=_params("arbitrary"),
        name="qkv_proj",
    )(x, gq.reshape(1, D_MODEL), gkv.reshape(1, D_MODEL), wq, wkv, head_mean,
      jnp.tile(qn, N_HEADS).reshape(1, D_MODEL), jnp.tile(kn, N_HEADS).reshape(1, D_MODEL))


def _attend_rows(q_rows, k_band, v_band, bias_ref, valid):
    lane = lax.broadcasted_iota(jnp.int32, (1, 2 * HEAD_DIM), 1)
    first = lane < HEAD_DIM
    outs = []
    for hp in range(N_HEADS // 2):
        sl = slice(hp * 2 * HEAD_DIM, (hp + 1) * 2 * HEAD_DIM)
        q2, k2, v2 = q_rows[:, sl], k_band[:, sl], v_band[:, sl]
        o_pair = []
        for e in range(2):
            qm = jnp.where(first if e == 0 else jnp.logical_not(first), q2, 0.0).astype(BF16)
            s = lax.dot_general(qm, k2, NT_DIMS, preferred_element_type=F32)
            s = s * (HEAD_DIM ** -0.5) + bias_ref[2 * hp + e]
            if valid is not None:
                s = jnp.where(valid, s, MASKED)
            p = jnp.exp(s - jnp.max(s, axis=-1, keepdims=True))
            den = jnp.sum(p, axis=-1, keepdims=True)
            o_pair.append(jnp.dot(p.astype(BF16), v2, preferred_element_type=F32) / den)
        outs.append(jnp.where(first, o_pair[0], o_pair[1]))
    return jnp.concatenate(outs, axis=-1)


def _attn_prompt_kernel(qb, q_ref, kp_ref, kc_ref, vp_ref, vc_ref, bias_ref, wo_ref, x_ref, o_ref,
                        kcat, vcat, oscr):
    i = pl.program_id(0)
    kcat[:qb] = kp_ref[...].astype(BF16)
    kcat[qb:] = kc_ref[...].astype(BF16)
    vcat[:qb] = vp_ref[...].astype(BF16)
    vcat[qb:] = vc_ref[...].astype(BF16)
    col = lax.broadcasted_iota(jnp.int32, (1, BAND), 1)

    def chunk(j, carry):
        r0 = pl.multiple_of(j * CHUNK, CHUNK)
        k0 = pl.multiple_of(qb - BAND_PAST + j * CHUNK, CHUNK)
        valid = (i * qb + j * CHUNK - BAND_PAST + col) >= 0
        o = _attend_rows(q_ref[pl.ds(r0, CHUNK), :], kcat[pl.ds(k0, BAND), :], vcat[pl.ds(k0, BAND), :],
                         bias_ref, valid)
        oscr[pl.ds(r0, CHUNK), :] = o.astype(BF16)
        return carry

    lax.fori_loop(0, qb // CHUNK, chunk, 0)
    o_ref[...] = x_ref[...] + jnp.dot(oscr[...], wo_ref[...], preferred_element_type=F32)


def _attn_prompt(x, q, k, v, bias, wo, qb):
    t = x.shape[0]
    cur = pl.BlockSpec((qb, D_MODEL), lambda i: (i, 0))
    prev = pl.BlockSpec((qb, D_MODEL), lambda i: (jnp.maximum(i - 1, 0), 0))
    return pl.pallas_call(
        functools.partial(_attn_prompt_kernel, qb),
        grid=(t // qb,),
        in_specs=[cur, prev, cur, prev, cur,
                  pl.BlockSpec((N_HEADS, CHUNK, BAND), lambda i: (0, 0, 0)),
                  pl.BlockSpec((D_MODEL, D_MODEL), lambda i: (0, 0)),
                  cur],
        out_specs=cur,
        out_shape=jax.ShapeDtypeStruct((t, D_MODEL), F32),
        scratch_shapes=[pltpu.VMEM((2 * qb, D_MODEL), BF16), pltpu.VMEM((2 * qb, D_MODEL), BF16),
                        pltpu.VMEM((qb, D_MODEL), BF16)],
        compiler_params=_params("arbitrary"),
        name="attn_prompt",
    )(q, k, k, v, v, bias, wo, x)


def _attn_sample_kernel(rows, q_ref, ck_ref, kn_ref, cv_ref, vn_ref, bias_ref, wo_ref, x_ref, o_ref,
                        kcat, vcat):
    kcat[:rows] = ck_ref[0].astype(BF16)
    kcat[rows:] = kn_ref[0].astype(BF16)
    vcat[:rows] = cv_ref[0].astype(BF16)
    vcat[rows:] = vn_ref[0].astype(BF16)
    o = _attend_rows(q_ref[0], kcat[...], vcat[...], bias_ref, None)
    o_ref[0] = x_ref[0] + jnp.dot(o.astype(BF16), wo_ref[...], preferred_element_type=F32)


def _attn_sample(x, q, k, v, cache_k, cache_v, bias, wo):
    bsz, t, _ = x.shape
    rows = cache_k.shape[1]
    new = pl.BlockSpec((1, t, D_MODEL), lambda b: (b, 0, 0))
    old = pl.BlockSpec((1, rows, D_MODEL), lambda b: (b, 0, 0))
    return pl.pallas_call(
        functools.partial(_attn_sample_kernel, rows),
        grid=(bsz,),
        in_specs=[new, old, new, old, new,
                  pl.BlockSpec((N_HEADS, t, rows + t), lambda b: (0, 0, 0)),
                  pl.BlockSpec((D_MODEL, D_MODEL), lambda b: (0, 0)),
                  new],
        out_specs=new,
        out_shape=jax.ShapeDtypeStruct((bsz, t, D_MODEL), F32),
        scratch_shapes=[pltpu.VMEM((rows + t, D_MODEL), BF16), pltpu.VMEM((rows + t, D_MODEL), BF16)],
        compiler_params=_params("arbitrary"),
        name="attn_sample",
    )(q, cache_k, k, cache_v, v, bias, wo, x)


def _band_bias(rel_bias, n_q, n_k, valid=None):
    rel = np.arange(n_q)[:, None] - np.arange(n_k)[None, :] + BAND_PAST
    idx = np.clip(rel, -REL_CLIP, REL_CLIP) + REL_CLIP
    bias = rel_bias[:, idx]
    if valid is not None:
        bias = jnp.where(valid[None], bias, MASKED)
    return bias


def _trunk(x, hist, cache_k, cache_v, pos0, p, tt, tb_sel, tb_dense, tb_qkv, qb):
    bsz, t, _ = x.shape
    n = bsz * t
    x1, st = _pool_mixer(x, hist, pos0, p["norm_mix"][0], p["pool_w"][0], p["pool_scale"][0], tt)
    x1 = x1.reshape(n, D_MODEL)
    x2 = _peer(x1, p["norm_ffn"][0], p["wq_t"][0], p["peer_keys"][0], p["u"][0], p["v_t"][0],
               tb_sel, tb_dense)
    q, k, v = _qkv(x2, p["norm_mix"][1], p["kv_norm"], p["w_q"], p["w_kv"], p["q_norm"], p["k_norm"], tb_qkv)
    if cache_k is None:
        assert bsz == 1 and t % qb == 0 and qb >= BAND_PAST and pos0 == 0
        x3 = _attn_prompt(x2, q, k, v, _band_bias(p["rel_bias"], CHUNK, BAND), p["w_o"], qb)
    else:
        rows = cache_k.shape[1]
        assert pos0 % CHUNK == 0 and t <= CHUNK and rows == BAND_PAST and pos0 >= rows
        bias = _band_bias(p["rel_bias"], t, rows + t)
        x3 = _attn_sample(x2.reshape(bsz, t, D_MODEL), q.reshape(bsz, t, D_MODEL),
                          k.reshape(bsz, t, D_MODEL), v.reshape(bsz, t, D_MODEL),
                          cache_k.reshape(bsz, rows, D_MODEL), cache_v.reshape(bsz, rows, D_MODEL),
                          bias, p["w_o"]).reshape(n, D_MODEL)
    x4 = _peer(x3, p["norm_ffn"][1], p["wq_t"][1], p["peer_keys"][1], p["u"][1], p["v_t"][1],
               tb_sel, tb_dense)
    return (x4.reshape(bsz, t, D_MODEL), st[None],
            k.reshape(bsz, t, N_HEADS, HEAD_DIM), v.reshape(bsz, t, N_HEADS, HEAD_DIM))


def _prepare(norm_mix, norm_ffn, pool_w, pool_scale, kv_norm, w_kv, k_norm, w_q, q_norm, rel_bias, w_o,
             peer_wq, peer_keys, peer_u, peer_v):
    assert w_q.shape[0] == 1 and pool_w.shape[0] == 1
    return dict(
        norm_mix=norm_mix, norm_ffn=norm_ffn, pool_w=pool_w, pool_scale=pool_scale, kv_norm=kv_norm,
        k_norm=k_norm, q_norm=q_norm[0], rel_bias=rel_bias[0], peer_keys=peer_keys,
        w_kv=w_kv.astype(BF16), w_q=w_q[0].astype(BF16), w_o=w_o[0].astype(BF16),
        wq_t=jnp.swapaxes(peer_wq, 1, 2).astype(BF16),
        u=peer_u.astype(BF16), v_t=jnp.swapaxes(peer_v, 1, 2).astype(BF16))


def kernel(x_prompt, x_sample, state_pool, cache_k, cache_v, norm_mix, norm_ffn, pool_w, pool_scale, kv_norm, w_kv, k_norm, w_q, q_norm, rel_bias, w_o, peer_wq, peer_keys, peer_u, peer_v):
    p = _prepare(norm_mix, norm_ffn, pool_w, pool_scale, kv_norm, w_kv, k_norm, w_q, q_norm, rel_bias, w_o,
                 peer_wq, peer_keys, peer_u, peer_v)
    hist0 = jnp.zeros((x_prompt.shape[0], POOL_HIST, D_MODEL), x_prompt.dtype)
    y_p, pool_p, k_p, v_p = _trunk(x_prompt, hist0, None, None, 0, p,
                                   tt=512, tb_sel=256, tb_dense=512, tb_qkv=512, qb=512)
    past_len = 4096
    y_s, pool_s, k_s, v_s = _trunk(x_sample, state_pool[0], cache_k, cache_v, past_len, p,
                                   tt=x_sample.shape[1], tb_sel=128, tb_dense=128, tb_qkv=128, qb=None)
    keep = min(BAND_PAST, x_prompt.shape[1])
    return (y_p, y_s, pool_p, pool_s, k_p[:, -keep:], v_p[:, -keep:], k_s, v_s)
```

```python
import functools

import numpy as np
import jax
import jax.numpy as jnp
from jax import lax
from jax.experimental import pallas as pl
from jax.experimental.pallas import tpu as pltpu

D_MODEL = 1024
CHUNK = 64
POOL_WINDOWS = (2, 4, 8, 16)
POOL_GROUP = D_MODEL // len(POOL_WINDOWS)
POOL_HIST = max(POOL_WINDOWS) - 1
HALO = POOL_HIST + 1
N_HEADS = 16
HEAD_DIM = 64
PAST_CHUNKS = 8
BAND_PAST = PAST_CHUNKS * CHUNK
BAND = BAND_PAST + CHUNK
REL_CLIP = 256
PEER_HEADS = 8
PEER_NKEYS = 128
PEER_N = PEER_NKEYS * PEER_NKEYS
PEER_HALF = 128
PEER_TOPK = 16
EPS = 1e-6
NOT_RANKED = 127.0
NEG_INF = float("-inf")
MASKED = -1e30

VMEM_LIMIT_BYTES = 56 * 1024 * 1024

F32 = jnp.float32
BF16 = jnp.bfloat16
NT_DIMS = (((1,), (1,)), ((), ()))


def _rms(x, g):
    return x * lax.rsqrt(jnp.mean(x * x, axis=-1, keepdims=True) + EPS) * g


def _params(*sem):
    return pltpu.CompilerParams(dimension_semantics=sem, vmem_limit_bytes=VMEM_LIMIT_BYTES)


def _pool_kernel(pos0, tt, n_t, x_ref, xp_ref, h_ref, g_ref, w_ref, sc_ref, o_ref, st_ref):
    i = pl.program_id(1)
    g = g_ref[...]
    x = x_ref[0]
    xn = _rms(x, g)
    halo = jnp.where(i == 0, h_ref[0], _rms(xp_ref[0], g))
    buf = jnp.concatenate([halo, xn], axis=0)
    pos = pos0 + i * tt + lax.broadcasted_iota(jnp.int32, (tt, 1), 0)
    outs = []
    for gi, wnd in enumerate(POOL_WINDOWS):
        sl = slice(gi * POOL_GROUP, (gi + 1) * POOL_GROUP)
        s = buf[:, sl]
        sh = 1
        while sh < wnd:
            s = s + pltpu.roll(s, sh, axis=0)
            sh *= 2
        cnt = jnp.minimum(pos + 1, wnd).astype(F32)
        pooled = s[HALO:] / cnt - xn[:, sl]
        outs.append(jnp.dot(pooled, w_ref[gi], precision=lax.Precision.HIGHEST,
                            preferred_element_type=F32))
    mix = jnp.concatenate(outs, axis=-1) * sc_ref[...]
    o_ref[0] = x + mix

    @pl.when(i == n_t - 1)
    def _():
        st_ref[0] = buf[tt:]


def _pool_mixer(x, hist, pos0, g, w, scale, tt):
    bsz, t, _ = x.shape
    n_t = t // tt
    hist16 = jnp.pad(hist, ((0, 0), (HALO - POOL_HIST, 0), (0, 0)))
    per_tt = tt // HALO
    out, st = pl.pallas_call(
        functools.partial(_pool_kernel, pos0, tt, n_t),
        grid=(bsz, n_t),
        in_specs=[
            pl.BlockSpec((1, tt, D_MODEL), lambda b, i: (b, i, 0)),
            pl.BlockSpec((1, HALO, D_MODEL), lambda b, i: (b, jnp.maximum(i * per_tt - 1, 0), 0)),
            pl.BlockSpec((1, HALO, D_MODEL), lambda b, i: (b, 0, 0)),
            pl.BlockSpec((1, D_MODEL), lambda b, i: (0, 0)),
            pl.BlockSpec((len(POOL_WINDOWS), POOL_GROUP, POOL_GROUP), lambda b, i: (0, 0, 0)),
            pl.BlockSpec((1, D_MODEL), lambda b, i: (0, 0)),
        ],
        out_specs=[
            pl.BlockSpec((1, tt, D_MODEL), lambda b, i: (b, i, 0)),
            pl.BlockSpec((1, HALO, D_MODEL), lambda b, i: (b, 0, 0)),
        ],
        out_shape=[
            jax.ShapeDtypeStruct((bsz, t, D_MODEL), F32),
            jax.ShapeDtypeStruct((bsz, HALO, D_MODEL), F32),
        ],
        compiler_params=_params("arbitrary", "arbitrary"),
        name="pool_mixer",
    )(x, x, hist16, g.reshape(1, D_MODEL), w, scale.reshape(1, D_MODEL))
    return out, st[:, HALO - POOL_HIST:]


def _top_ranks(s):
    rank = jnp.full(s.shape, NOT_RANKED, F32)
    vals = []
    for k in range(PEER_TOPK):
        m = jnp.max(s, axis=0, keepdims=True)
        hit = s == m
        rank = jnp.where(hit, float(k), rank)
        s = jnp.where(hit, NEG_INF, s)
        vals.append(m)
    return vals, rank


def _rows_to_block(rows):
    n = len(rows)
    rid = lax.broadcasted_iota(jnp.int32, (n, rows[0].shape[1]), 0)
    blk = jnp.zeros((n, rows[0].shape[1]), F32)
    for k, r in enumerate(rows):
        blk = jnp.where(rid == k, r, blk)
    return blk


def _bf16_pair(x):
    hi = pltpu.bitcast(x.astype(BF16).astype(F32), jnp.uint32)
    return hi | (hi >> 16)


def _select_kernel(x_ref, g_ref, wq_ref, keys_ref, xn_ref, r2_ref, l_ref, c_ref, e2_ref):
    h = pl.program_id(1)

    @pl.when(h == 0)
    def _():
        xn_ref[...] = _rms(x_ref[...], g_ref[...]).astype(BF16)

    q_t = lax.dot_general(wq_ref[...], xn_ref[...], NT_DIMS, preferred_element_type=F32)
    s1 = jnp.dot(keys_ref[0, 0], q_t[:PEER_HALF], precision=lax.Precision.HIGHEST,
                 preferred_element_type=F32)
    s2 = jnp.dot(keys_ref[0, 1], q_t[PEER_HALF:], precision=lax.Precision.HIGHEST,
                 preferred_element_type=F32)
    v1, rank1 = _top_ranks(s1)
    v2, rank2 = _top_ranks(s2)
    v2_blk = _rows_to_block(v2)
    rid = lax.broadcasted_iota(jnp.int32, (F32_ROWS, v2_blk.shape[1]), 0)
    cands = [v1[0] + v2_blk]
    for r1 in range(1, F32_ROWS):
        cands.append(jnp.where(rid < PEER_TOPK // (r1 + 1), v1[r1] + v2_blk[:F32_ROWS], NEG_INF))
    assert PEER_TOPK // (F32_ROWS + 1) == 1
    tail = _rows_to_block(v1[F32_ROWS:]) + v2[0]
    x = jnp.concatenate(cands + [tail], axis=0)
    tops = []
    for _ in range(PEER_TOPK):
        m = jnp.max(x, axis=0, keepdims=True)
        x = jnp.where(x == m, NEG_INF, x)
        tops.append(m)
    tau = tops[-1]
    z = jnp.zeros_like(tau)
    for m in tops:
        z = z + jnp.exp(m - tops[0])
    l_of_i1 = jnp.zeros(s1.shape, F32)
    for r1 in range(PEER_TOPK):
        if r1 < F32_ROWS:
            n_sel = jnp.sum((cands[r1] >= tau).astype(F32), axis=0, keepdims=True)
        else:
            n_sel = ((v1[r1] + v2[0]) >= tau).astype(F32)
        l_of_i1 = jnp.where(rank1 == float(r1), n_sel, l_of_i1)
    r2_ref[0] = rank2.astype(BF16)
    l_ref[0] = _bf16_pair(l_of_i1)
    c_ref[0] = _bf16_pair(jnp.exp(s1 - v1[0]) / z)
    e2_ref[0] = jnp.exp(s2 - v2[0]).astype(BF16)


def _peer_select(x, g, wq_t, keys, tb):
    t = x.shape[0]
    pair = jax.ShapeDtypeStruct((PEER_HEADS, PEER_NKEYS, t), jnp.uint32)
    feat16 = jax.ShapeDtypeStruct((PEER_HEADS, PEER_NKEYS, t), BF16)
    feat_spec = pl.BlockSpec((1, PEER_NKEYS, tb), lambda i, h: (h, 0, i))
    return pl.pallas_call(
        _select_kernel,
        grid=(t // tb, PEER_HEADS),
        in_specs=[
            pl.BlockSpec((tb, D_MODEL), lambda i, h: (i, 0)),
            pl.BlockSpec((1, D_MODEL), lambda i, h: (0, 0)),
            pl.BlockSpec((2 * PEER_HALF, D_MODEL), lambda i, h: (h, 0)),
            pl.BlockSpec((1, 2, PEER_NKEYS, PEER_HALF), lambda i, h: (h, 0, 0, 0)),
        ],
        out_specs=[pl.BlockSpec((tb, D_MODEL), lambda i, h: (i, 0)),
                   feat_spec, feat_spec, feat_spec, feat_spec],
        out_shape=[jax.ShapeDtypeStruct((t, D_MODEL), BF16), feat16, pair, pair, feat16],
        compiler_params=_params("arbitrary", "arbitrary"),
        name="peer_select",
    )(x, g.reshape(1, D_MODEL), wq_t, keys)


I1_PER_CHUNK = 16
E_CHUNK = I1_PER_CHUNK * PEER_NKEYS


BF16_ROWS = 16
F32_ROWS = 8


def _packed_rows(word_row):
    return pltpu.bitcast(jnp.broadcast_to(word_row, (F32_ROWS, word_row.shape[1])), BF16)


def _dense_kernel(n_chunks, x_ref, xn_ref, u_ref, vt_ref, r2_ref, e2_ref, l_ref, c_ref, o_ref, acc_ref):
    c = pl.program_id(1)

    @pl.when(c == 0)
    def _():
        acc_ref[...] = jnp.zeros_like(acc_ref)

    a_t = lax.dot_general(u_ref[...], xn_ref[...], NT_DIMS, preferred_element_type=F32)
    pieces = []
    for j in range(I1_PER_CHUNK):
        rows = [(_packed_rows(l_ref[h, j:j + 1, :]), _packed_rows(c_ref[h, j:j + 1, :]))
                for h in range(PEER_HEADS)]
        for b in range(PEER_NKEYS // BF16_ROWS):
            i2 = slice(b * BF16_ROWS, (b + 1) * BF16_ROWS)
            gate = None
            for h in range(PEER_HEADS):
                l_rows, c_rows = rows[h]
                term = jnp.where(r2_ref[h, i2, :] < l_rows, e2_ref[h, i2, :] * c_rows, jnp.zeros((), BF16))
                gate = term if gate is None else gate + term
            r0 = j * PEER_NKEYS + b * BF16_ROWS
            a_b = a_t[r0:r0 + BF16_ROWS].astype(BF16)
            act = 0.5 * a_b * (1.0 + lax.erf(a_b * (2.0 ** -0.5)))
            pieces.append(act * gate)
    h_t = jnp.concatenate(pieces, axis=0)
    acc_ref[...] += jnp.dot(vt_ref[...], h_t, preferred_element_type=F32)

    @pl.when(c == n_chunks - 1)
    def _():
        o_ref[...] = x_ref[...] + acc_ref[...].T


def _peer_dense(x, xn, u, v_t, r2, e2, l, cc, tb):
    t = x.shape[0]
    n_chunks = PEER_N // E_CHUNK
    tok = pl.BlockSpec((tb, D_MODEL), lambda i, c: (i, 0))
    full = pl.BlockSpec((PEER_HEADS, PEER_NKEYS, tb), lambda i, c: (0, 0, i))
    part = pl.BlockSpec((PEER_HEADS, I1_PER_CHUNK, tb), lambda i, c: (0, c, i))
    return pl.pallas_call(
        functools.partial(_dense_kernel, n_chunks),
        grid=(t // tb, n_chunks),
        in_specs=[tok, tok,
                  pl.BlockSpec((E_CHUNK, D_MODEL), lambda i, c: (c, 0)),
                  pl.BlockSpec((D_MODEL, E_CHUNK), lambda i, c: (0, c)),
                  full, full, part, part],
        out_specs=tok,
        out_shape=jax.ShapeDtypeStruct((t, D_MODEL), F32),
        scratch_shapes=[pltpu.VMEM((D_MODEL, tb), F32)],
        compiler_params=_params("arbitrary", "arbitrary"),
        name="peer_dense",
    )(x, xn, u, v_t, r2, e2, l, cc)


def _peer(x, g, wq_t, keys, u, v_t, tb_sel, tb_dense):
    xn, r2, l, cc, e2 = _peer_select(x, g, wq_t, keys, tb_sel)
    return _peer_dense(x, xn, u, v_t, r2, e2, l, cc, tb_dense)


def _head_norm(z, head_mean, gn):
    zz = z * z
    hi = zz.astype(BF16)
    lo = (zz - hi.astype(F32)).astype(BF16)
    ms = (jnp.dot(hi, head_mean, preferred_element_type=F32)
          + jnp.dot(lo, head_mean, preferred_element_type=F32))
    return z * lax.rsqrt(ms + EPS) * gn


def _qkv_kernel(x_ref, gq_ref, gkv_ref, wq_ref, wkv_ref, hm_ref, qn_ref, kn_ref, q_ref, k_ref, v_ref):
    x = x_ref[...]
    hm = hm_ref[...]
    q = jnp.dot(_rms(x, gq_ref[...]).astype(BF16), wq_ref[...], preferred_element_type=F32)
    kv = jnp.dot(_rms(x, gkv_ref[...]).astype(BF16), wkv_ref[...], preferred_element_type=F32)
    q_ref[...] = _head_norm(q, hm, qn_ref[...])
    k_ref[...] = _head_norm(kv[:, :D_MODEL], hm, kn_ref[...])
    v_ref[...] = kv[:, D_MODEL:]


def _qkv(x, gq, gkv, wq, wkv, qn, kn, tb):
    t = x.shape[0]
    head_id = np.arange(D_MODEL) // HEAD_DIM
    head_mean = jnp.asarray((head_id[:, None] == head_id[None, :]) / HEAD_DIM, BF16)
    tok = pl.BlockSpec((tb, D_MODEL), lambda i: (i, 0))
    vec = pl.BlockSpec((1, D_MODEL), lambda i: (0, 0))
    sq = pl.BlockSpec((D_MODEL, D_MODEL), lambda i: (0, 0))
    out = jax.ShapeDtypeStruct((t, D_MODEL), F32)
    return pl.pallas_call(
        _qkv_kernel,
        grid=(t // tb,),
        in_specs=[tok, vec, vec, sq, pl.BlockSpec((D_MODEL, 2 * D_MODEL), lambda i: (0, 0)), sq, vec, vec],
        out_specs=[tok, tok, tok],
        out_shape=[out, out, out],
        compiler_params=_params("arbitrary"),
        name="qkv_proj",
    )(x, gq.reshape(1, D_MODEL), gkv.reshape(1, D_MODEL), wq, wkv, head_mean,
      jnp.tile(qn, N_HEADS).reshape(1, D_MODEL), jnp.tile(kn, N_HEADS).reshape(1, D_MODEL))


def _attend_rows(q_rows, k_band, v_band, bias_ref, valid):
    r = q_rows.shape[0]
    lane = lax.broadcasted_iota(jnp.int32, (1, 2 * HEAD_DIM), 1)
    first = lane < HEAD_DIM
    outs = []
    for hp in range(N_HEADS // 2):
        sl = slice(hp * 2 * HEAD_DIM, (hp + 1) * 2 * HEAD_DIM)
        q2, k2, v2 = q_rows[:, sl], k_band[:, sl], v_band[:, sl]
        qm = jnp.concatenate([jnp.where(first, q2, 0.0), jnp.where(first, 0.0, q2)], axis=0).astype(BF16)
        s = lax.dot_general(qm, k2, NT_DIMS, preferred_element_type=F32)
        s = s * (HEAD_DIM ** -0.5) + bias_ref[hp]
        if valid is not None:
            s = jnp.where(valid, s, MASKED)
        p = jnp.exp(s - jnp.max(s, axis=-1, keepdims=True))
        den = jnp.sum(p, axis=-1, keepdims=True)
        o2 = jnp.dot(p.astype(BF16), v2, preferred_element_type=F32) / den
        outs.append(jnp.where(first, o2[:r], o2[r:]))
    return jnp.concatenate(outs, axis=-1)


def _attn_prompt_kernel(qb, q_ref, kp_ref, kc_ref, vp_ref, vc_ref, bias_ref, wo_ref, x_ref, o_ref,
                        kcat, vcat, oscr):
    i = pl.program_id(0)
    kcat[:qb] = kp_ref[...].astype(BF16)
    kcat[qb:] = kc_ref[...].astype(BF16)
    vcat[:qb] = vp_ref[...].astype(BF16)
    vcat[qb:] = vc_ref[...].astype(BF16)
    col = lax.broadcasted_iota(jnp.int32, (1, BAND), 1)

    def chunk(j, carry):
        r0 = pl.multiple_of(j * CHUNK, CHUNK)
        k0 = pl.multiple_of(qb - BAND_PAST + j * CHUNK, CHUNK)
        valid = (i * qb + j * CHUNK - BAND_PAST + col) >= 0
        o = _attend_rows(q_ref[pl.ds(r0, CHUNK), :], kcat[pl.ds(k0, BAND), :], vcat[pl.ds(k0, BAND), :],
                         bias_ref, valid)
        oscr[pl.ds(r0, CHUNK), :] = o.astype(BF16)
        return carry

    lax.fori_loop(0, qb // CHUNK, chunk, 0)
    o_ref[...] = x_ref[...] + jnp.dot(oscr[...], wo_ref[...], preferred_element_type=F32)


def _attn_prompt(x, q, k, v, bias, wo, qb):
    t = x.shape[0]
    cur = pl.BlockSpec((qb, D_MODEL), lambda i: (i, 0))
    prev = pl.BlockSpec((qb, D_MODEL), lambda i: (jnp.maximum(i - 1, 0), 0))
    return pl.pallas_call(
        functools.partial(_attn_prompt_kernel, qb),
        grid=(t // qb,),
        in_specs=[cur, prev, cur, prev, cur,
                  pl.BlockSpec((N_HEADS // 2, 2 * CHUNK, BAND), lambda i: (0, 0, 0)),
                  pl.BlockSpec((D_MODEL, D_MODEL), lambda i: (0, 0)),
                  cur],
        out_specs=cur,
        out_shape=jax.ShapeDtypeStruct((t, D_MODEL), F32),
        scratch_shapes=[pltpu.VMEM((2 * qb, D_MODEL), BF16), pltpu.VMEM((2 * qb, D_MODEL), BF16),
                        pltpu.VMEM((qb, D_MODEL), BF16)],
        compiler_params=_params("arbitrary"),
        name="attn_prompt",
    )(q, k, k, v, v, bias, wo, x)


def _attn_sample_kernel(rows, q_ref, ck_ref, kn_ref, cv_ref, vn_ref, bias_ref, wo_ref, x_ref, o_ref,
                        kcat, vcat):
    kcat[:rows] = ck_ref[0].astype(BF16)
    kcat[rows:] = kn_ref[0].astype(BF16)
    vcat[:rows] = cv_ref[0].astype(BF16)
    vcat[rows:] = vn_ref[0].astype(BF16)
    o = _attend_rows(q_ref[0], kcat[...], vcat[...], bias_ref, None)
    o_ref[0] = x_ref[0] + jnp.dot(o.astype(BF16), wo_ref[...], preferred_element_type=F32)


def _attn_sample(x, q, k, v, cache_k, cache_v, bias, wo):
    bsz, t, _ = x.shape
    rows = cache_k.shape[1]
    new = pl.BlockSpec((1, t, D_MODEL), lambda b: (b, 0, 0))
    old = pl.BlockSpec((1, rows, D_MODEL), lambda b: (b, 0, 0))
    return pl.pallas_call(
        functools.partial(_attn_sample_kernel, rows),
        grid=(bsz,),
        in_specs=[new, old, new, old, new,
                  pl.BlockSpec((N_HEADS // 2, 2 * t, rows + t), lambda b: (0, 0, 0)),
                  pl.BlockSpec((D_MODEL, D_MODEL), lambda b: (0, 0)),
                  new],
        out_specs=new,
        out_shape=jax.ShapeDtypeStruct((bsz, t, D_MODEL), F32),
        scratch_shapes=[pltpu.VMEM((rows + t, D_MODEL), BF16), pltpu.VMEM((rows + t, D_MODEL), BF16)],
        compiler_params=_params("arbitrary"),
        name="attn_sample",
    )(q, cache_k, k, cache_v, v, bias, wo, x)


def _band_bias(rel_bias, n_q, n_k):
    m = np.arange(n_q - 1 + n_k)
    diag = rel_bias[:, np.clip(BAND_PAST + n_q - 1 - m, -REL_CLIP, REL_CLIP) + REL_CLIP]
    bias = jnp.stack([diag[:, n_q - 1 - qo:n_q - 1 - qo + n_k] for qo in range(n_q)], axis=1)
    return bias.reshape(N_HEADS // 2, 2 * n_q, n_k)


def _trunk(x, hist, cache_k, cache_v, pos0, p, tt, tb_sel, tb_dense, tb_qkv, qb):
    bsz, t, _ = x.shape
    n = bsz * t
    x1, st = _pool_mixer(x, hist, pos0, p["norm_mix"][0], p["pool_w"][0], p["pool_scale"][0], tt)
    x1 = x1.reshape(n, D_MODEL)
    x2 = _peer(x1, p["norm_ffn"][0], p["wq_t"][0], p["peer_keys"][0], p["u"][0], p["v_t"][0],
               tb_sel, tb_dense)
    q, k, v = _qkv(x2, p["norm_mix"][1], p["kv_norm"], p["w_q"], p["w_kv"], p["q_norm"], p["k_norm"], tb_qkv)
    if cache_k is None:
        assert bsz == 1 and t % qb == 0 and qb >= BAND_PAST and pos0 == 0
        x3 = _attn_prompt(x2, q, k, v, _band_bias(p["rel_bias"], CHUNK, BAND), p["w_o"], qb)
    else:
        rows = cache_k.shape[1]
        assert pos0 % CHUNK == 0 and t <= CHUNK and rows == BAND_PAST and pos0 >= rows
        bias = _band_bias(p["rel_bias"], t, rows + t)
        x3 = _attn_sample(x2.reshape(bsz, t, D_MODEL), q.reshape(bsz, t, D_MODEL),
                          k.reshape(bsz, t, D_MODEL), v.reshape(bsz, t, D_MODEL),
                          cache_k.reshape(bsz, rows, D_MODEL), cache_v.reshape(bsz, rows, D_MODEL),
                          bias, p["w_o"]).reshape(n, D_MODEL)
    x4 = _peer(x3, p["norm_ffn"][1], p["wq_t"][1], p["peer_keys"][1], p["u"][1], p["v_t"][1],
               tb_sel, tb_dense)
    return (x4.reshape(bsz, t, D_MODEL), st[None],
            k.reshape(bsz, t, N_HEADS, HEAD_DIM), v.reshape(bsz, t, N_HEADS, HEAD_DIM))


def _prepare(norm_mix, norm_ffn, pool_w, pool_scale, kv_norm, w_kv, k_norm, w_q, q_norm, rel_bias, w_o,
             peer_wq, peer_keys, peer_u, peer_v):
    assert w_q.shape[0] == 1 and pool_w.shape[0] == 1
    return dict(
        norm_mix=norm_mix, norm_ffn=norm_ffn, pool_w=pool_w, pool_scale=pool_scale, kv_norm=kv_norm,
        k_norm=k_norm, q_norm=q_norm[0], rel_bias=rel_bias[0], peer_keys=peer_keys,
        w_kv=w_kv.astype(BF16), w_q=w_q[0].astype(BF16), w_o=w_o[0].astype(BF16),
        wq_t=jnp.swapaxes(peer_wq, 1, 2).astype(BF16),
        u=peer_u.astype(BF16), v_t=jnp.swapaxes(peer_v, 1, 2).astype(BF16))


def kernel(x_prompt, x_sample, state_pool, cache_k, cache_v, norm_mix, norm_ffn, pool_w, pool_scale, kv_norm, w_kv, k_norm, w_q, q_norm, rel_bias, w_o, peer_wq, peer_keys, peer_u, peer_v):
    p = _prepare(norm_mix, norm_ffn, pool_w, pool_scale, kv_norm, w_kv, k_norm, w_q, q_norm, rel_bias, w_o,
                 peer_wq, peer_keys, peer_u, peer_v)
    hist0 = jnp.zeros((x_prompt.shape[0], POOL_HIST, D_MODEL), x_prompt.dtype)
    y_p, pool_p, k_p, v_p = _trunk(x_prompt, hist0, None, None, 0, p,
                                   tt=512, tb_sel=256, tb_dense=512, tb_qkv=512, qb=512)
    past_len = 4096
    y_s, pool_s, k_s, v_s = _trunk(x_sample, state_pool[0], cache_k, cache_v, past_len, p,
                                   tt=x_sample.shape[1], tb_sel=128, tb_dense=128, tb_qkv=128, qb=None)
    keep = min(BAND_PAST, x_prompt.shape[1])
    return (y_p, y_s, pool_p, pool_s, k_p[:, -keep:], v_p[:, -keep:], k_s, v_s)
```

```python
import functools

import numpy as np
import jax
import jax.numpy as jnp
from jax import lax
from jax.experimental import pallas as pl
from jax.experimental.pallas import tpu as pltpu

D_MODEL = 1024
CHUNK = 64
POOL_WINDOWS = (2, 4, 8, 16)
POOL_GROUP = D_MODEL // len(POOL_WINDOWS)
POOL_HIST = max(POOL_WINDOWS) - 1
HALO = POOL_HIST + 1
N_HEADS = 16
HEAD_DIM = 64
PAST_CHUNKS = 8
BAND_PAST = PAST_CHUNKS * CHUNK
BAND = BAND_PAST + CHUNK
REL_CLIP = 256
PEER_HEADS = 8
PEER_NKEYS = 128
PEER_N = PEER_NKEYS * PEER_NKEYS
PEER_HALF = 128
PEER_TOPK = 16
EPS = 1e-6
NOT_RANKED = 127.0
NEG_INF = float("-inf")
MASKED = -1e30

VMEM_LIMIT_BYTES = 56 * 1024 * 1024

F32 = jnp.float32
BF16 = jnp.bfloat16
NT_DIMS = (((1,), (1,)), ((), ()))


def _rms(x, g):
    return x * lax.rsqrt(jnp.mean(x * x, axis=-1, keepdims=True) + EPS) * g


def _params(*sem):
    return pltpu.CompilerParams(dimension_semantics=sem, vmem_limit_bytes=VMEM_LIMIT_BYTES)


def _pool_kernel(pos0, tt, n_t, x_ref, xp_ref, h_ref, g_ref, w_ref, sc_ref, o_ref, st_ref):
    i = pl.program_id(1)
    g = g_ref[...]
    x = x_ref[0]
    xn = _rms(x, g)
    halo = jnp.where(i == 0, h_ref[0], _rms(xp_ref[0], g))
    buf = jnp.concatenate([halo, xn], axis=0)
    pos = pos0 + i * tt + lax.broadcasted_iota(jnp.int32, (tt, 1), 0)
    outs = []
    for gi, wnd in enumerate(POOL_WINDOWS):
        sl = slice(gi * POOL_GROUP, (gi + 1) * POOL_GROUP)
        s = buf[:, sl]
        sh = 1
        while sh < wnd:
            s = s + pltpu.roll(s, sh, axis=0)
            sh *= 2
        cnt = jnp.minimum(pos + 1, wnd).astype(F32)
        pooled = s[HALO:] / cnt - xn[:, sl]
        outs.append(jnp.dot(pooled, w_ref[gi], precision=lax.Precision.HIGHEST,
                            preferred_element_type=F32))
    mix = jnp.concatenate(outs, axis=-1) * sc_ref[...]
    o_ref[0] = x + mix

    @pl.when(i == n_t - 1)
    def _():
        st_ref[0] = buf[tt:]


def _pool_mixer(x, hist, pos0, g, w, scale, tt):
    bsz, t, _ = x.shape
    n_t = t // tt
    hist16 = jnp.pad(hist, ((0, 0), (HALO - POOL_HIST, 0), (0, 0)))
    per_tt = tt // HALO
    out, st = pl.pallas_call(
        functools.partial(_pool_kernel, pos0, tt, n_t),
        grid=(bsz, n_t),
        in_specs=[
            pl.BlockSpec((1, tt, D_MODEL), lambda b, i: (b, i, 0)),
            pl.BlockSpec((1, HALO, D_MODEL), lambda b, i: (b, jnp.maximum(i * per_tt - 1, 0), 0)),
            pl.BlockSpec((1, HALO, D_MODEL), lambda b, i: (b, 0, 0)),
            pl.BlockSpec((1, D_MODEL), lambda b, i: (0, 0)),
            pl.BlockSpec((len(POOL_WINDOWS), POOL_GROUP, POOL_GROUP), lambda b, i: (0, 0, 0)),
            pl.BlockSpec((1, D_MODEL), lambda b, i: (0, 0)),
        ],
        out_specs=[
            pl.BlockSpec((1, tt, D_MODEL), lambda b, i: (b, i, 0)),
            pl.BlockSpec((1, HALO, D_MODEL), lambda b, i: (b, 0, 0)),
        ],
        out_shape=[
            jax.ShapeDtypeStruct((bsz, t, D_MODEL), F32),
            jax.ShapeDtypeStruct((bsz, HALO, D_MODEL), F32),
        ],
        compiler_params=_params("arbitrary", "arbitrary"),
        name="pool_mixer",
    )(x, x, hist16, g.reshape(1, D_MODEL), w, scale.reshape(1, D_MODEL))
    return out, st[:, HALO - POOL_HIST:]


def _top_ranks(s):
    rank = jnp.full(s.shape, NOT_RANKED, F32)
    vals = []
    for k in range(PEER_TOPK):
        m = jnp.max(s, axis=0, keepdims=True)
        hit = s == m
        rank = jnp.where(hit, float(k), rank)
        s = jnp.where(hit, NEG_INF, s)
        vals.append(m)
    return vals, rank


def _rows_to_block(rows):
    n = len(rows)
    rid = lax.broadcasted_iota(jnp.int32, (n, rows[0].shape[1]), 0)
    blk = jnp.zeros((n, rows[0].shape[1]), F32)
    for k, r in enumerate(rows):
        blk = jnp.where(rid == k, r, blk)
    return blk


def _bf16_pair(x):
    hi = pltpu.bitcast(x.astype(BF16).astype(F32), jnp.uint32)
    return hi | (hi >> 16)


def _fold_kernel(keys_ref, wq_ref, o_ref):
    o_ref[...] = lax.dot_general(keys_ref[0, 0], wq_ref[...], NT_DIMS, precision=lax.Precision.HIGHEST,
                                 preferred_element_type=F32).astype(BF16)


def _fold_keys(keys, wq):
    return pl.pallas_call(
        _fold_kernel,
        grid=(PEER_HEADS, 2),
        in_specs=[pl.BlockSpec((1, 1, PEER_NKEYS, PEER_HALF), lambda h, p: (h, p, 0, 0)),
                  pl.BlockSpec((D_MODEL, PEER_HALF), lambda h, p: (0, 2 * h + p))],
        out_specs=pl.BlockSpec((PEER_NKEYS, D_MODEL), lambda h, p: (2 * h + p, 0)),
        out_shape=jax.ShapeDtypeStruct((PEER_HEADS * 2 * PEER_NKEYS, D_MODEL), BF16),
        compiler_params=_params("arbitrary", "arbitrary"),
        name="fold_keys",
    )(keys, wq)


def _select_kernel(x_ref, g_ref, wk_ref, xn_ref, r2_ref, l_ref, c_ref, e2_ref):
    h = pl.program_id(1)

    @pl.when(h == 0)
    def _():
        xn_ref[...] = _rms(x_ref[...], g_ref[...]).astype(BF16)

    s = lax.dot_general(wk_ref[...], xn_ref[...], NT_DIMS, preferred_element_type=F32)
    s1, s2 = s[:PEER_NKEYS], s[PEER_NKEYS:]
    v1, rank1 = _top_ranks(s1)
    v2, rank2 = _top_ranks(s2)
    v2_blk = _rows_to_block(v2)
    rid = lax.broadcasted_iota(jnp.int32, (F32_ROWS, v2_blk.shape[1]), 0)
    cands = [v1[0] + v2_blk]
    for r1 in range(1, F32_ROWS):
        cands.append(jnp.where(rid < PEER_TOPK // (r1 + 1), v1[r1] + v2_blk[:F32_ROWS], NEG_INF))
    assert PEER_TOPK // (F32_ROWS + 1) == 1
    tail = _rows_to_block(v1[F32_ROWS:]) + v2[0]
    x = jnp.concatenate(cands + [tail], axis=0)
    tops = []
    for _ in range(PEER_TOPK):
        m = jnp.max(x, axis=0, keepdims=True)
        x = jnp.where(x == m, NEG_INF, x)
        tops.append(m)
    tau = tops[-1]
    z = jnp.zeros_like(tau)
    for m in tops:
        z = z + jnp.exp(m - tops[0])
    l_of_i1 = jnp.zeros(s1.shape, F32)
    for r1 in range(PEER_TOPK):
        if r1 < F32_ROWS:
            n_sel = jnp.sum((cands[r1] >= tau).astype(F32), axis=0, keepdims=True)
        else:
            n_sel = ((v1[r1] + v2[0]) >= tau).astype(F32)
        l_of_i1 = jnp.where(rank1 == float(r1), n_sel, l_of_i1)
    r2_ref[0] = rank2.astype(BF16)
    l_ref[0] = _bf16_pair(l_of_i1)
    c_ref[0] = _bf16_pair(jnp.exp(s1 - v1[0]) / z)
    e2_ref[0] = jnp.exp(s2 - v2[0]).astype(BF16)


def _peer_select(x, g, wk, tb):
    t = x.shape[0]
    pair = jax.ShapeDtypeStruct((PEER_HEADS, PEER_NKEYS, t), jnp.uint32)
    feat16 = jax.ShapeDtypeStruct((PEER_HEADS, PEER_NKEYS, t), BF16)
    feat_spec = pl.BlockSpec((1, PEER_NKEYS, tb), lambda i, h: (h, 0, i))
    return pl.pallas_call(
        _select_kernel,
        grid=(t // tb, PEER_HEADS),
        in_specs=[
            pl.BlockSpec((tb, D_MODEL), lambda i, h: (i, 0)),
            pl.BlockSpec((1, D_MODEL), lambda i, h: (0, 0)),
            pl.BlockSpec((2 * PEER_NKEYS, D_MODEL), lambda i, h: (h, 0)),
        ],
        out_specs=[pl.BlockSpec((tb, D_MODEL), lambda i, h: (i, 0)),
                   feat_spec, feat_spec, feat_spec, feat_spec],
        out_shape=[jax.ShapeDtypeStruct((t, D_MODEL), BF16), feat16, pair, pair, feat16],
        compiler_params=_params("arbitrary", "arbitrary"),
        name="peer_select",
    )(x, g.reshape(1, D_MODEL), wk)


I1_PER_CHUNK = 16
E_CHUNK = I1_PER_CHUNK * PEER_NKEYS


BF16_ROWS = 16
F32_ROWS = 8


def _packed_rows(word_row):
    return pltpu.bitcast(jnp.broadcast_to(word_row, (F32_ROWS, word_row.shape[1])), BF16)


def _dense_kernel(n_chunks, x_ref, xn_ref, u_ref, vt_ref, r2_ref, e2_ref, l_ref, c_ref, o_ref, acc_ref):
    c = pl.program_id(1)

    @pl.when(c == 0)
    def _():
        acc_ref[...] = jnp.zeros_like(acc_ref)

    a_t = lax.dot_general(u_ref[...], xn_ref[...], NT_DIMS, preferred_element_type=F32)
    pieces = []
    for j in range(I1_PER_CHUNK):
        rows = [(_packed_rows(l_ref[h, j:j + 1, :]), _packed_rows(c_ref[h, j:j + 1, :]))
                for h in range(PEER_HEADS)]
        for b in range(PEER_NKEYS // BF16_ROWS):
            i2 = slice(b * BF16_ROWS, (b + 1) * BF16_ROWS)
            gate = None
            for h in range(PEER_HEADS):
                l_rows, c_rows = rows[h]
                term = jnp.where(r2_ref[h, i2, :] < l_rows, e2_ref[h, i2, :] * c_rows, jnp.zeros((), BF16))
                gate = term if gate is None else gate + term
            r0 = j * PEER_NKEYS + b * BF16_ROWS
            a_b = a_t[r0:r0 + BF16_ROWS].astype(BF16)
            act = 0.5 * a_b * (1.0 + lax.erf(a_b * (2.0 ** -0.5)))
            pieces.append(act * gate)
    h_t = jnp.concatenate(pieces, axis=0)
    acc_ref[...] += jnp.dot(vt_ref[...], h_t, preferred_element_type=F32)

    @pl.when(c == n_chunks - 1)
    def _():
        o_ref[...] = x_ref[...] + acc_ref[...].T


def _peer_dense(x, xn, u, v_t, r2, e2, l, cc, tb):
    t = x.shape[0]
    n_chunks = PEER_N // E_CHUNK
    tok = pl.BlockSpec((tb, D_MODEL), lambda i, c: (i, 0))
    full = pl.BlockSpec((PEER_HEADS, PEER_NKEYS, tb), lambda i, c: (0, 0, i))
    part = pl.BlockSpec((PEER_HEADS, I1_PER_CHUNK, tb), lambda i, c: (0, c, i))
    return pl.pallas_call(
        functools.partial(_dense_kernel, n_chunks),
        grid=(t // tb, n_chunks),
        in_specs=[tok, tok,
                  pl.BlockSpec((E_CHUNK, D_MODEL), lambda i, c: (c, 0)),
                  pl.BlockSpec((D_MODEL, E_CHUNK), lambda i, c: (0, c)),
                  full, full, part, part],
        out_specs=tok,
        out_shape=jax.ShapeDtypeStruct((t, D_MODEL), F32),
        scratch_shapes=[pltpu.VMEM((D_MODEL, tb), F32)],
        compiler_params=_params("arbitrary", "arbitrary"),
        name="peer_dense",
    )(x, xn, u, v_t, r2, e2, l, cc)


def _peer(x, g, wk, u, v_t, tb_sel, tb_dense):
    xn, r2, l, cc, e2 = _peer_select(x, g, wk, tb_sel)
    return _peer_dense(x, xn, u, v_t, r2, e2, l, cc, tb_dense)


def _head_norm(z, head_mean, gn):
    zz = z * z
    hi = zz.astype(BF16)
    lo = (zz - hi.astype(F32)).astype(BF16)
    ms = (jnp.dot(hi, head_mean, preferred_element_type=F32)
          + jnp.dot(lo, head_mean, preferred_element_type=F32))
    return z * lax.rsqrt(ms + EPS) * gn


def _qkv_kernel(x_ref, gq_ref, gkv_ref, wq_ref, wkv_ref, hm_ref, qn_ref, kn_ref, q_ref, k_ref, v_ref):
    x = x_ref[...]
    hm = hm_ref[...]
    q = jnp.dot(_rms(x, gq_ref[...]).astype(BF16), wq_ref[...], preferred_element_type=F32)
    kv = jnp.dot(_rms(x, gkv_ref[...]).astype(BF16), wkv_ref[...], preferred_element_type=F32)
    q_ref[...] = _head_norm(q, hm, qn_ref[...])
    k_ref[...] = _head_norm(kv[:, :D_MODEL], hm, kn_ref[...])
    v_ref[...] = kv[:, D_MODEL:]


def _qkv(x, gq, gkv, wq, wkv, qn, kn, tb):
    t = x.shape[0]
    head_id = np.arange(D_MODEL) // HEAD_DIM
    head_mean = jnp.asarray((head_id[:, None] == head_id[None, :]) / HEAD_DIM, BF16)
    tok = pl.BlockSpec((tb, D_MODEL), lambda i: (i, 0))
    vec = pl.BlockSpec((1, D_MODEL), lambda i: (0, 0))
    sq = pl.BlockSpec((D_MODEL, D_MODEL), lambda i: (0, 0))
    out = jax.ShapeDtypeStruct((t, D_MODEL), F32)
    return pl.pallas_call(
        _qkv_kernel,
        grid=(t // tb,),
        in_specs=[tok, vec, vec, sq, pl.BlockSpec((D_MODEL, 2 * D_MODEL), lambda i: (0, 0)), sq, vec, vec],
        out_specs=[tok, tok, tok],
        out_shape=[out, out, out],
        compiler_params=_params("arbitrary"),
        name="qkv_proj",
    )(x, gq.reshape(1, D_MODEL), gkv.reshape(1, D_MODEL), wq, wkv, head_mean,
      jnp.tile(qn, N_HEADS).reshape(1, D_MODEL), jnp.tile(kn, N_HEADS).reshape(1, D_MODEL))


def _attend_rows(q_rows, k_band, v_band, bias_ref, valid):
    r = q_rows.shape[0]
    lane = lax.broadcasted_iota(jnp.int32, (1, 2 * HEAD_DIM), 1)
    first = lane < HEAD_DIM
    pairs = [slice(hp * 2 * HEAD_DIM, (hp + 1) * 2 * HEAD_DIM) for hp in range(N_HEADS // 2)]

    def scores(hp):
        q2 = q_rows[:, pairs[hp]]
        qm = jnp.concatenate([jnp.where(first, q2, 0.0), jnp.where(first, 0.0, q2)], axis=0).astype(BF16)
        s = lax.dot_general(qm, k_band[:, pairs[hp]], NT_DIMS, preferred_element_type=F32)
        s = s * (HEAD_DIM ** -0.5) + bias_ref[hp]
        return s if valid is None else jnp.where(valid, s, MASKED)

    def attend(hp, s):
        p = jnp.exp(s - jnp.max(s, axis=-1, keepdims=True))
        den = jnp.sum(p, axis=-1, keepdims=True)
        o2 = jnp.dot(p.astype(BF16), v_band[:, pairs[hp]], preferred_element_type=F32) / den
        return jnp.where(first, o2[:r], o2[r:])

    outs = []
    s_next = scores(0)
    for hp in range(len(pairs)):
        s_cur = s_next
        if hp + 1 < len(pairs):
            s_next = scores(hp + 1)
        outs.append(attend(hp, s_cur))
    return jnp.concatenate(outs, axis=-1)


def _attn_prompt_kernel(qb, q_ref, kp_ref, kc_ref, vp_ref, vc_ref, bias_ref, wo_ref, x_ref, o_ref,
                        kcat, vcat, oscr):
    i = pl.program_id(0)
    kcat[:qb] = kp_ref[...].astype(BF16)
    kcat[qb:] = kc_ref[...].astype(BF16)
    vcat[:qb] = vp_ref[...].astype(BF16)
    vcat[qb:] = vc_ref[...].astype(BF16)
    col = lax.broadcasted_iota(jnp.int32, (1, BAND), 1)

    def chunk(j, carry):
        r0 = pl.multiple_of(j * CHUNK, CHUNK)
        k0 = pl.multiple_of(qb - BAND_PAST + j * CHUNK, CHUNK)
        valid = (i * qb + j * CHUNK - BAND_PAST + col) >= 0
        o = _attend_rows(q_ref[pl.ds(r0, CHUNK), :], kcat[pl.ds(k0, BAND), :], vcat[pl.ds(k0, BAND), :],
                         bias_ref, valid)
        oscr[pl.ds(r0, CHUNK), :] = o.astype(BF16)
        return carry

    lax.fori_loop(0, qb // CHUNK, chunk, 0)
    o_ref[...] = x_ref[...] + jnp.dot(oscr[...], wo_ref[...], preferred_element_type=F32)


def _attn_prompt(x, q, k, v, bias, wo, qb):
    t = x.shape[0]
    cur = pl.BlockSpec((qb, D_MODEL), lambda i: (i, 0))
    prev = pl.BlockSpec((qb, D_MODEL), lambda i: (jnp.maximum(i - 1, 0), 0))
    return pl.pallas_call(
        functools.partial(_attn_prompt_kernel, qb),
        grid=(t // qb,),
        in_specs=[cur, prev, cur, prev, cur,
                  pl.BlockSpec((N_HEADS // 2, 2 * CHUNK, BAND), lambda i: (0, 0, 0)),
                  pl.BlockSpec((D_MODEL, D_MODEL), lambda i: (0, 0)),
                  cur],
        out_specs=cur,
        out_shape=jax.ShapeDtypeStruct((t, D_MODEL), F32),
        scratch_shapes=[pltpu.VMEM((2 * qb, D_MODEL), BF16), pltpu.VMEM((2 * qb, D_MODEL), BF16),
                        pltpu.VMEM((qb, D_MODEL), BF16)],
        compiler_params=_params("arbitrary"),
        name="attn_prompt",
    )(q, k, k, v, v, bias, wo, x)


def _attn_sample_kernel(rows, q_ref, ck_ref, kn_ref, cv_ref, vn_ref, bias_ref, wo_ref, x_ref, o_ref,
                        kcat, vcat):
    kcat[:rows] = ck_ref[0].astype(BF16)
    kcat[rows:] = kn_ref[0].astype(BF16)
    vcat[:rows] = cv_ref[0].astype(BF16)
    vcat[rows:] = vn_ref[0].astype(BF16)
    o = _attend_rows(q_ref[0], kcat[...], vcat[...], bias_ref, None)
    o_ref[0] = x_ref[0] + jnp.dot(o.astype(BF16), wo_ref[...], preferred_element_type=F32)


def _attn_sample(x, q, k, v, cache_k, cache_v, bias, wo):
    bsz, t, _ = x.shape
    rows = cache_k.shape[1]
    new = pl.BlockSpec((1, t, D_MODEL), lambda b: (b, 0, 0))
    old = pl.BlockSpec((1, rows, D_MODEL), lambda b: (b, 0, 0))
    return pl.pallas_call(
        functools.partial(_attn_sample_kernel, rows),
        grid=(bsz,),
        in_specs=[new, old, new, old, new,
                  pl.BlockSpec((N_HEADS // 2, 2 * t, rows + t), lambda b: (0, 0, 0)),
                  pl.BlockSpec((D_MODEL, D_MODEL), lambda b: (0, 0)),
                  new],
        out_specs=new,
        out_shape=jax.ShapeDtypeStruct((bsz, t, D_MODEL), F32),
        scratch_shapes=[pltpu.VMEM((rows + t, D_MODEL), BF16), pltpu.VMEM((rows + t, D_MODEL), BF16)],
        compiler_params=_params("arbitrary"),
        name="attn_sample",
    )(q, cache_k, k, cache_v, v, bias, wo, x)


def _band_bias(rel_bias, n_q, n_k):
    m = np.arange(n_q - 1 + n_k)
    diag = rel_bias[:, np.clip(BAND_PAST + n_q - 1 - m, -REL_CLIP, REL_CLIP) + REL_CLIP]
    bias = jnp.stack([diag[:, n_q - 1 - qo:n_q - 1 - qo + n_k] for qo in range(n_q)], axis=1)
    return bias.reshape(N_HEADS // 2, 2 * n_q, n_k)


def _trunk(x, hist, cache_k, cache_v, pos0, p, tt, tb_sel, tb_dense, tb_qkv, qb):
    bsz, t, _ = x.shape
    n = bsz * t
    x1, st = _pool_mixer(x, hist, pos0, p["norm_mix"][0], p["pool_w"][0], p["pool_scale"][0], tt)
    x1 = x1.reshape(n, D_MODEL)
    x2 = _peer(x1, p["norm_ffn"][0], p["wk"][0], p["u"][0], p["v_t"][0],
               tb_sel, tb_dense)
    q, k, v = _qkv(x2, p["norm_mix"][1], p["kv_norm"], p["w_q"], p["w_kv"], p["q_norm"], p["k_norm"], tb_qkv)
    if cache_k is None:
        assert bsz == 1 and t % qb == 0 and qb >= BAND_PAST and pos0 == 0
        x3 = _attn_prompt(x2, q, k, v, _band_bias(p["rel_bias"], CHUNK, BAND), p["w_o"], qb)
    else:
        rows = cache_k.shape[1]
        assert pos0 % CHUNK == 0 and t <= CHUNK and rows == BAND_PAST and pos0 >= rows
        bias = _band_bias(p["rel_bias"], t, rows + t)
        x3 = _attn_sample(x2.reshape(bsz, t, D_MODEL), q.reshape(bsz, t, D_MODEL),
                          k.reshape(bsz, t, D_MODEL), v.reshape(bsz, t, D_MODEL),
                          cache_k.reshape(bsz, rows, D_MODEL), cache_v.reshape(bsz, rows, D_MODEL),
                          bias, p["w_o"]).reshape(n, D_MODEL)
    x4 = _peer(x3, p["norm_ffn"][1], p["wk"][1], p["u"][1], p["v_t"][1],
               tb_sel, tb_dense)
    return (x4.reshape(bsz, t, D_MODEL), st[None],
            k.reshape(bsz, t, N_HEADS, HEAD_DIM), v.reshape(bsz, t, N_HEADS, HEAD_DIM))


def _prepare(norm_mix, norm_ffn, pool_w, pool_scale, kv_norm, w_kv, k_norm, w_q, q_norm, rel_bias, w_o,
             peer_wq, peer_keys, peer_u, peer_v):
    assert w_q.shape[0] == 1 and pool_w.shape[0] == 1
    return dict(
        norm_mix=norm_mix, norm_ffn=norm_ffn, pool_w=pool_w, pool_scale=pool_scale, kv_norm=kv_norm,
        k_norm=k_norm, q_norm=q_norm[0], rel_bias=rel_bias[0],
        w_kv=w_kv.astype(BF16), w_q=w_q[0].astype(BF16), w_o=w_o[0].astype(BF16),
        wk=[_fold_keys(peer_keys[i], peer_wq[i]) for i in range(peer_wq.shape[0])],
        u=peer_u.astype(BF16), v_t=jnp.swapaxes(peer_v, 1, 2).astype(BF16))


def kernel(x_prompt, x_sample, state_pool, cache_k, cache_v, norm_mix, norm_ffn, pool_w, pool_scale, kv_norm, w_kv, k_norm, w_q, q_norm, rel_bias, w_o, peer_wq, peer_keys, peer_u, peer_v):
    p = _prepare(norm_mix, norm_ffn, pool_w, pool_scale, kv_norm, w_kv, k_norm, w_q, q_norm, rel_bias, w_o,
                 peer_wq, peer_keys, peer_u, peer_v)
    hist0 = jnp.zeros((x_prompt.shape[0], POOL_HIST, D_MODEL), x_prompt.dtype)
    y_p, pool_p, k_p, v_p = _trunk(x_prompt, hist0, None, None, 0, p,
                                   tt=512, tb_sel=256, tb_dense=512, tb_qkv=512, qb=512)
    past_len = 4096
    y_s, pool_s, k_s, v_s = _trunk(x_sample, state_pool[0], cache_k, cache_v, past_len, p,
                                   tt=x_sample.shape[1], tb_sel=128, tb_dense=128, tb_qkv=128, qb=None)
    keep = min(BAND_PAST, x_prompt.shape[1])
    return (y_p, y_s, pool_p, pool_s, k_p[:, -keep:], v_p[:, -keep:], k_s, v_s)
```

```python
import functools

import numpy as np
import jax
import jax.numpy as jnp
from jax import lax
from jax.experimental import pallas as pl
from jax.experimental.pallas import tpu as pltpu

D_MODEL = 1024
CHUNK = 64
POOL_WINDOWS = (2, 4, 8, 16)
POOL_GROUP = D_MODEL // len(POOL_WINDOWS)
POOL_HIST = max(POOL_WINDOWS) - 1
HALO = POOL_HIST + 1
N_HEADS = 16
HEAD_DIM = 64
PAST_CHUNKS = 8
BAND_PAST = PAST_CHUNKS * CHUNK
BAND = BAND_PAST + CHUNK
REL_CLIP = 256
PEER_HEADS = 8
PEER_NKEYS = 128
PEER_N = PEER_NKEYS * PEER_NKEYS
PEER_HALF = 128
PEER_TOPK = 16
EPS = 1e-6
NOT_RANKED = 127.0
NEG_INF = float("-inf")
MASKED = -1e30

VMEM_LIMIT_BYTES = 56 * 1024 * 1024

F32 = jnp.float32
BF16 = jnp.bfloat16
NT_DIMS = (((1,), (1,)), ((), ()))


def _rms(x, g):
    return x * lax.rsqrt(jnp.mean(x * x, axis=-1, keepdims=True) + EPS) * g


def _params(*sem):
    return pltpu.CompilerParams(dimension_semantics=sem, vmem_limit_bytes=VMEM_LIMIT_BYTES)


def _pool_kernel(pos0, tt, n_t, x_ref, xp_ref, h_ref, g_ref, w_ref, sc_ref, o_ref, st_ref):
    i = pl.program_id(1)
    g = g_ref[...]
    x = x_ref[0]
    xn = _rms(x, g)
    halo = jnp.where(i == 0, h_ref[0], _rms(xp_ref[0], g))
    buf = jnp.concatenate([halo, xn], axis=0)
    pos = pos0 + i * tt + lax.broadcasted_iota(jnp.int32, (tt, 1), 0)
    outs = []
    for gi, wnd in enumerate(POOL_WINDOWS):
        sl = slice(gi * POOL_GROUP, (gi + 1) * POOL_GROUP)
        s = buf[:, sl]
        sh = 1
        while sh < wnd:
            s = s + pltpu.roll(s, sh, axis=0)
            sh *= 2
        cnt = jnp.minimum(pos + 1, wnd).astype(F32)
        pooled = s[HALO:] / cnt - xn[:, sl]
        outs.append(jnp.dot(pooled, w_ref[gi], precision=lax.Precision.HIGHEST,
                            preferred_element_type=F32))
    mix = jnp.concatenate(outs, axis=-1) * sc_ref[...]
    o_ref[0] = x + mix

    @pl.when(i == n_t - 1)
    def _():
        st_ref[0] = buf[tt:]


def _pool_mixer(x, hist, pos0, g, w, scale, tt):
    bsz, t, _ = x.shape
    n_t = t // tt
    hist16 = jnp.pad(hist, ((0, 0), (HALO - POOL_HIST, 0), (0, 0)))
    per_tt = tt // HALO
    out, st = pl.pallas_call(
        functools.partial(_pool_kernel, pos0, tt, n_t),
        grid=(bsz, n_t),
        in_specs=[
            pl.BlockSpec((1, tt, D_MODEL), lambda b, i: (b, i, 0)),
            pl.BlockSpec((1, HALO, D_MODEL), lambda b, i: (b, jnp.maximum(i * per_tt - 1, 0), 0)),
            pl.BlockSpec((1, HALO, D_MODEL), lambda b, i: (b, 0, 0)),
            pl.BlockSpec((1, D_MODEL), lambda b, i: (0, 0)),
            pl.BlockSpec((len(POOL_WINDOWS), POOL_GROUP, POOL_GROUP), lambda b, i: (0, 0, 0)),
            pl.BlockSpec((1, D_MODEL), lambda b, i: (0, 0)),
        ],
        out_specs=[
            pl.BlockSpec((1, tt, D_MODEL), lambda b, i: (b, i, 0)),
            pl.BlockSpec((1, HALO, D_MODEL), lambda b, i: (b, 0, 0)),
        ],
        out_shape=[
            jax.ShapeDtypeStruct((bsz, t, D_MODEL), F32),
            jax.ShapeDtypeStruct((bsz, HALO, D_MODEL), F32),
        ],
        compiler_params=_params("arbitrary", "arbitrary"),
        name="pool_mixer",
    )(x, x, hist16, g.reshape(1, D_MODEL), w, scale.reshape(1, D_MODEL))
    return out, st[:, HALO - POOL_HIST:]


def _top_values(s, want_rank):
    rank = jnp.full(s.shape, NOT_RANKED, F32) if want_rank else None
    vals = []
    for k in range(PEER_TOPK):
        m = jnp.max(s, axis=0, keepdims=True)
        hit = s == m
        if want_rank:
            rank = jnp.where(hit, float(k), rank)
        s = jnp.where(hit, NEG_INF, s)
        vals.append(m)
    return vals, rank


def _rows_to_block(rows):
    n = len(rows)
    rid = lax.broadcasted_iota(jnp.int32, (n, rows[0].shape[1]), 0)
    blk = jnp.zeros((n, rows[0].shape[1]), F32)
    for k, r in enumerate(rows):
        blk = jnp.where(rid == k, r, blk)
    return blk


def _bf16_pair(x):
    hi = pltpu.bitcast(x.astype(BF16).astype(F32), jnp.uint32)
    return hi | (hi >> 16)


def _fold_kernel(keys_ref, wq_ref, o_ref):
    o_ref[...] = lax.dot_general(keys_ref[0, 0], wq_ref[...], NT_DIMS, precision=lax.Precision.HIGHEST,
                                 preferred_element_type=F32).astype(BF16)


def _fold_keys(keys, wq):
    return pl.pallas_call(
        _fold_kernel,
        grid=(PEER_HEADS, 2),
        in_specs=[pl.BlockSpec((1, 1, PEER_NKEYS, PEER_HALF), lambda h, p: (h, p, 0, 0)),
                  pl.BlockSpec((D_MODEL, PEER_HALF), lambda h, p: (0, 2 * h + p))],
        out_specs=pl.BlockSpec((PEER_NKEYS, D_MODEL), lambda h, p: (2 * h + p, 0)),
        out_shape=jax.ShapeDtypeStruct((PEER_HEADS * 2 * PEER_NKEYS, D_MODEL), BF16),
        compiler_params=_params("arbitrary", "arbitrary"),
        name="fold_keys",
    )(keys, wq)


def _select_head(s, h, r2_ref, l_ref, c_ref, e2_ref):
    s1, s2 = s[:PEER_NKEYS], s[PEER_NKEYS:]
    v1, _ = _top_values(s1, False)
    v2, rank2 = _top_values(s2, True)
    v2_blk = _rows_to_block(v2)
    rid = lax.broadcasted_iota(jnp.int32, (F32_ROWS, v2_blk.shape[1]), 0)
    cands = [v1[0] + v2_blk]
    for r1 in range(1, F32_ROWS):
        cands.append(jnp.where(rid < PEER_TOPK // (r1 + 1), v1[r1] + v2_blk[:F32_ROWS], NEG_INF))
    assert PEER_TOPK // (F32_ROWS + 1) == 1
    tail = _rows_to_block(v1[F32_ROWS:]) + v2[0]
    x = jnp.concatenate(cands + [tail], axis=0)
    tops = []
    for _ in range(PEER_TOPK):
        m = jnp.max(x, axis=0, keepdims=True)
        x = jnp.where(x == m, NEG_INF, x)
        tops.append(m)
    tau = tops[-1]
    z = jnp.zeros_like(tau)
    for m in tops:
        z = z + jnp.exp(m - tops[0])
    l_of_i1 = jnp.zeros(s1.shape, F32)
    for r1 in range(PEER_TOPK):
        if r1 < F32_ROWS:
            n_sel = jnp.sum((cands[r1] >= tau).astype(F32), axis=0, keepdims=True)
        else:
            n_sel = ((v1[r1] + v2[0]) >= tau).astype(F32)
        l_of_i1 = jnp.where(s1 == v1[r1], n_sel, l_of_i1)
    r2_ref[h] = rank2.astype(BF16)
    l_ref[h] = _bf16_pair(l_of_i1)
    c_ref[h] = _bf16_pair(jnp.exp(s1 - v1[0]) / z)
    e2_ref[h] = jnp.exp(s2 - v2[0]).astype(BF16)


def _select_kernel(x_ref, g_ref, wk_ref, xn_ref, r2_ref, l_ref, c_ref, e2_ref):
    xn = _rms(x_ref[...], g_ref[...]).astype(BF16)
    xn_ref[...] = xn

    def scores(h):
        return lax.dot_general(wk_ref[h * 2 * PEER_NKEYS:(h + 1) * 2 * PEER_NKEYS, :], xn, NT_DIMS,
                               preferred_element_type=F32)

    s_next = scores(0)
    for h in range(PEER_HEADS):
        s = s_next
        if h + 1 < PEER_HEADS:
            s_next = scores(h + 1)
        _select_head(s, h, r2_ref, l_ref, c_ref, e2_ref)


def _peer_select(x, g, wk, tb):
    t = x.shape[0]
    pair = jax.ShapeDtypeStruct((PEER_HEADS, PEER_NKEYS, t), jnp.uint32)
    feat16 = jax.ShapeDtypeStruct((PEER_HEADS, PEER_NKEYS, t), BF16)
    feat_spec = pl.BlockSpec((PEER_HEADS, PEER_NKEYS, tb), lambda i: (0, 0, i))
    return pl.pallas_call(
        _select_kernel,
        grid=(t // tb,),
        in_specs=[
            pl.BlockSpec((tb, D_MODEL), lambda i: (i, 0)),
            pl.BlockSpec((1, D_MODEL), lambda i: (0, 0)),
            pl.BlockSpec((PEER_HEADS * 2 * PEER_NKEYS, D_MODEL), lambda i: (0, 0)),
        ],
        out_specs=[pl.BlockSpec((tb, D_MODEL), lambda i: (i, 0)),
                   feat_spec, feat_spec, feat_spec, feat_spec],
        out_shape=[jax.ShapeDtypeStruct((t, D_MODEL), BF16), feat16, pair, pair, feat16],
        compiler_params=_params("arbitrary"),
        name="peer_select",
    )(x, g.reshape(1, D_MODEL), wk)


I1_PER_CHUNK = 16
E_CHUNK = I1_PER_CHUNK * PEER_NKEYS


BF16_ROWS = 16
F32_ROWS = 8


def _packed_rows(word_row):
    return pltpu.bitcast(jnp.broadcast_to(word_row, (F32_ROWS, word_row.shape[1])), BF16)


def _dense_kernel(n_chunks, x_ref, xn_ref, u_ref, vt_ref, r2_ref, e2_ref, l_ref, c_ref, o_ref, acc_ref):
    c = pl.program_id(1)

    @pl.when(c == 0)
    def _():
        acc_ref[...] = jnp.zeros_like(acc_ref)

    a_t = lax.dot_general(u_ref[...], xn_ref[...], NT_DIMS, preferred_element_type=F32)
    pieces = []
    for j in range(I1_PER_CHUNK):
        rows = [(_packed_rows(l_ref[h, j:j + 1, :]), _packed_rows(c_ref[h, j:j + 1, :]))
                for h in range(PEER_HEADS)]
        for b in range(PEER_NKEYS // BF16_ROWS):
            i2 = slice(b * BF16_ROWS, (b + 1) * BF16_ROWS)
            gate = None
            for h in range(PEER_HEADS):
                l_rows, c_rows = rows[h]
                term = jnp.where(r2_ref[h, i2, :] < l_rows, e2_ref[h, i2, :] * c_rows, jnp.zeros((), BF16))
                gate = term if gate is None else gate + term
            r0 = j * PEER_NKEYS + b * BF16_ROWS
            a_b = a_t[r0:r0 + BF16_ROWS].astype(BF16)
            act = 0.5 * a_b * (1.0 + lax.erf(a_b * (2.0 ** -0.5)))
            pieces.append(act * gate)
    h_t = jnp.concatenate(pieces, axis=0)
    acc_ref[...] += jnp.dot(vt_ref[...], h_t, preferred_element_type=F32)

    @pl.when(c == n_chunks - 1)
    def _():
        o_ref[...] = x_ref[...] + acc_ref[...].T


def _peer_dense(x, xn, u, v_t, layer, r2, e2, l, cc, tb):
    t = x.shape[0]
    n_chunks = PEER_N // E_CHUNK
    tok = pl.BlockSpec((tb, D_MODEL), lambda i, c: (i, 0))
    full = pl.BlockSpec((PEER_HEADS, PEER_NKEYS, tb), lambda i, c: (0, 0, i))
    part = pl.BlockSpec((PEER_HEADS, I1_PER_CHUNK, tb), lambda i, c: (0, c, i))
    return pl.pallas_call(
        functools.partial(_dense_kernel, n_chunks),
        grid=(t // tb, n_chunks),
        in_specs=[tok, tok,
                  pl.BlockSpec((None, E_CHUNK, D_MODEL), lambda i, c: (layer, c, 0)),
                  pl.BlockSpec((None, D_MODEL, E_CHUNK), lambda i, c: (layer, 0, c)),
                  full, full, part, part],
        out_specs=tok,
        out_shape=jax.ShapeDtypeStruct((t, D_MODEL), F32),
        scratch_shapes=[pltpu.VMEM((D_MODEL, tb), F32)],
        compiler_params=_params("arbitrary", "arbitrary"),
        name="peer_dense",
    )(x, xn, u, v_t, r2, e2, l, cc)


def _peer(x, g, wk, u, v_t, layer, tb_sel, tb_dense):
    xn, r2, l, cc, e2 = _peer_select(x, g, wk, tb_sel)
    return _peer_dense(x, xn, u, v_t, layer, r2, e2, l, cc, tb_dense)


def _head_norm(z, head_mean, gn):
    zz = z * z
    hi = zz.astype(BF16)
    lo = (zz - hi.astype(F32)).astype(BF16)
    ms = (jnp.dot(hi, head_mean, preferred_element_type=F32)
          + jnp.dot(lo, head_mean, preferred_element_type=F32))
    return z * lax.rsqrt(ms + EPS) * gn


def _qkv_kernel(x_ref, gq_ref, gkv_ref, wq_ref, wkv_ref, hm_ref, qn_ref, kn_ref, q_ref, k_ref, v_ref):
    x = x_ref[...]
    hm = hm_ref[...]
    q = jnp.dot(_rms(x, gq_ref[...]).astype(BF16), wq_ref[...], preferred_element_type=F32)
    kv = jnp.dot(_rms(x, gkv_ref[...]).astype(BF16), wkv_ref[...], preferred_element_type=F32)
    q_ref[...] = _head_norm(q, hm, qn_ref[...])
    k_ref[...] = _head_norm(kv[:, :D_MODEL], hm, kn_ref[...])
    v_ref[...] = kv[:, D_MODEL:]


def _qkv(x, gq, gkv, wq, wkv, qn, kn, tb):
    t = x.shape[0]
    head_id = np.arange(D_MODEL) // HEAD_DIM
    head_mean = jnp.asarray((head_id[:, None] == head_id[None, :]) / HEAD_DIM, BF16)
    tok = pl.BlockSpec((tb, D_MODEL), lambda i: (i, 0))
    vec = pl.BlockSpec((1, D_MODEL), lambda i: (0, 0))
    sq = pl.BlockSpec((D_MODEL, D_MODEL), lambda i: (0, 0))
    out = jax.ShapeDtypeStruct((t, D_MODEL), F32)
    return pl.pallas_call(
        _qkv_kernel,
        grid=(t // tb,),
        in_specs=[tok, vec, vec, sq, pl.BlockSpec((D_MODEL, 2 * D_MODEL), lambda i: (0, 0)), sq, vec, vec],
        out_specs=[tok, tok, tok],
        out_shape=[out, out, out],
        compiler_params=_params("arbitrary"),
        name="qkv_proj",
    )(x, gq.reshape(1, D_MODEL), gkv.reshape(1, D_MODEL), wq, wkv, head_mean,
      jnp.tile(qn, N_HEADS).reshape(1, D_MODEL), jnp.tile(kn, N_HEADS).reshape(1, D_MODEL))


def _attend_rows(q_rows, k_band, v_band, bias_ref, valid):
    r = q_rows.shape[0]
    lane = lax.broadcasted_iota(jnp.int32, (1, 2 * HEAD_DIM), 1)
    first = lane < HEAD_DIM
    pairs = [slice(hp * 2 * HEAD_DIM, (hp + 1) * 2 * HEAD_DIM) for hp in range(N_HEADS // 2)]

    def scores(hp):
        q2 = q_rows[:, pairs[hp]]
        qm = jnp.concatenate([jnp.where(first, q2, 0.0), jnp.where(first, 0.0, q2)], axis=0).astype(BF16)
        s = lax.dot_general(qm, k_band[:, pairs[hp]], NT_DIMS, preferred_element_type=F32)
        s = s * (HEAD_DIM ** -0.5) + bias_ref[hp]
        return s if valid is None else jnp.where(valid, s, MASKED)

    def attend(hp, s):
        p = jnp.exp(s - jnp.max(s, axis=-1, keepdims=True))
        den = jnp.sum(p, axis=-1, keepdims=True)
        o2 = jnp.dot(p.astype(BF16), v_band[:, pairs[hp]], preferred_element_type=F32) / den
        return jnp.where(first, o2[:r], o2[r:])

    outs = []
    s_next = scores(0)
    for hp in range(len(pairs)):
        s_cur = s_next
        if hp + 1 < len(pairs):
            s_next = scores(hp + 1)
        outs.append(attend(hp, s_cur))
    return jnp.concatenate(outs, axis=-1)


def _attn_prompt_kernel(qb, q_ref, kp_ref, kc_ref, vp_ref, vc_ref, bias_ref, wo_ref, x_ref, o_ref,
                        kcat, vcat, oscr):
    i = pl.program_id(0)
    kcat[:qb] = kp_ref[...].astype(BF16)
    kcat[qb:] = kc_ref[...].astype(BF16)
    vcat[:qb] = vp_ref[...].astype(BF16)
    vcat[qb:] = vc_ref[...].astype(BF16)
    col = lax.broadcasted_iota(jnp.int32, (1, BAND), 1)

    def chunk(j, carry):
        r0 = pl.multiple_of(j * CHUNK, CHUNK)
        k0 = pl.multiple_of(qb - BAND_PAST + j * CHUNK, CHUNK)
        valid = (i * qb + j * CHUNK - BAND_PAST + col) >= 0
        o = _attend_rows(q_ref[pl.ds(r0, CHUNK), :], kcat[pl.ds(k0, BAND), :], vcat[pl.ds(k0, BAND), :],
                         bias_ref, valid)
        oscr[pl.ds(r0, CHUNK), :] = o.astype(BF16)
        return carry

    lax.fori_loop(0, qb // CHUNK, chunk, 0)
    o_ref[...] = x_ref[...] + jnp.dot(oscr[...], wo_ref[...], preferred_element_type=F32)


def _attn_prompt(x, q, k, v, bias, wo, qb):
    t = x.shape[0]
    cur = pl.BlockSpec((qb, D_MODEL), lambda i: (i, 0))
    prev = pl.BlockSpec((qb, D_MODEL), lambda i: (jnp.maximum(i - 1, 0), 0))
    return pl.pallas_call(
        functools.partial(_attn_prompt_kernel, qb),
        grid=(t // qb,),
        in_specs=[cur, prev, cur, prev, cur,
                  pl.BlockSpec((N_HEADS // 2, 2 * CHUNK, BAND), lambda i: (0, 0, 0)),
                  pl.BlockSpec((D_MODEL, D_MODEL), lambda i: (0, 0)),
                  cur],
        out_specs=cur,
        out_shape=jax.ShapeDtypeStruct((t, D_MODEL), F32),
        scratch_shapes=[pltpu.VMEM((2 * qb, D_MODEL), BF16), pltpu.VMEM((2 * qb, D_MODEL), BF16),
                        pltpu.VMEM((qb, D_MODEL), BF16)],
        compiler_params=_params("arbitrary"),
        name="attn_prompt",
    )(q, k, k, v, v, bias, wo, x)


def _attn_sample_kernel(rows, q_ref, ck_ref, kn_ref, cv_ref, vn_ref, bias_ref, wo_ref, x_ref, o_ref,
                        kcat, vcat):
    kcat[:rows] = ck_ref[0].astype(BF16)
    kcat[rows:] = kn_ref[0].astype(BF16)
    vcat[:rows] = cv_ref[0].astype(BF16)
    vcat[rows:] = vn_ref[0].astype(BF16)
    o = _attend_rows(q_ref[0], kcat[...], vcat[...], bias_ref, None)
    o_ref[0] = x_ref[0] + jnp.dot(o.astype(BF16), wo_ref[...], preferred_element_type=F32)


def _attn_sample(x, q, k, v, cache_k, cache_v, bias, wo):
    bsz, t, _ = x.shape
    rows = cache_k.shape[1]
    new = pl.BlockSpec((1, t, D_MODEL), lambda b: (b, 0, 0))
    old = pl.BlockSpec((1, rows, D_MODEL), lambda b: (b, 0, 0))
    return pl.pallas_call(
        functools.partial(_attn_sample_kernel, rows),
        grid=(bsz,),
        in_specs=[new, old, new, old, new,
                  pl.BlockSpec((N_HEADS // 2, 2 * t, rows + t), lambda b: (0, 0, 0)),
                  pl.BlockSpec((D_MODEL, D_MODEL), lambda b: (0, 0)),
                  new],
        out_specs=new,
        out_shape=jax.ShapeDtypeStruct((bsz, t, D_MODEL), F32),
        scratch_shapes=[pltpu.VMEM((rows + t, D_MODEL), BF16), pltpu.VMEM((rows + t, D_MODEL), BF16)],
        compiler_params=_params("arbitrary"),
        name="attn_sample",
    )(q, cache_k, k, cache_v, v, bias, wo, x)


def _band_bias(rel_bias, n_q, n_k):
    m = np.arange(n_q - 1 + n_k)
    diag = rel_bias[:, np.clip(BAND_PAST + n_q - 1 - m, -REL_CLIP, REL_CLIP) + REL_CLIP]
    bias = jnp.stack([diag[:, n_q - 1 - qo:n_q - 1 - qo + n_k] for qo in range(n_q)], axis=1)
    return bias.reshape(N_HEADS // 2, 2 * n_q, n_k)


def _trunk(x, hist, cache_k, cache_v, pos0, p, tt, tb_sel, tb_dense, tb_qkv, qb):
    bsz, t, _ = x.shape
    n = bsz * t
    x1, st = _pool_mixer(x, hist, pos0, p["norm_mix"][0], p["pool_w"][0], p["pool_scale"][0], tt)
    x1 = x1.reshape(n, D_MODEL)
    x2 = _peer(x1, p["norm_ffn"][0], p["wk"][0], p["u"], p["v_t"], 0,
               tb_sel, tb_dense)
    q, k, v = _qkv(x2, p["norm_mix"][1], p["kv_norm"], p["w_q"], p["w_kv"], p["q_norm"], p["k_norm"], tb_qkv)
    if cache_k is None:
        assert bsz == 1 and t % qb == 0 and qb >= BAND_PAST and pos0 == 0
        x3 = _attn_prompt(x2, q, k, v, _band_bias(p["rel_bias"], CHUNK, BAND), p["w_o"], qb)
    else:
        rows = cache_k.shape[1]
        assert pos0 % CHUNK == 0 and t <= CHUNK and rows == BAND_PAST and pos0 >= rows
        bias = _band_bias(p["rel_bias"], t, rows + t)
        x3 = _attn_sample(x2.reshape(bsz, t, D_MODEL), q.reshape(bsz, t, D_MODEL),
                          k.reshape(bsz, t, D_MODEL), v.reshape(bsz, t, D_MODEL),
                          cache_k.reshape(bsz, rows, D_MODEL), cache_v.reshape(bsz, rows, D_MODEL),
                          bias, p["w_o"]).reshape(n, D_MODEL)
    x4 = _peer(x3, p["norm_ffn"][1], p["wk"][1], p["u"], p["v_t"], 1,
               tb_sel, tb_dense)
    return (x4.reshape(bsz, t, D_MODEL), st[None],
            k.reshape(bsz, t, N_HEADS, HEAD_DIM), v.reshape(bsz, t, N_HEADS, HEAD_DIM))


def _prepare(norm_mix, norm_ffn, pool_w, pool_scale, kv_norm, w_kv, k_norm, w_q, q_norm, rel_bias, w_o,
             peer_wq, peer_keys, peer_u, peer_v):
    assert w_q.shape[0] == 1 and pool_w.shape[0] == 1
    return dict(
        norm_mix=norm_mix, norm_ffn=norm_ffn, pool_w=pool_w, pool_scale=pool_scale, kv_norm=kv_norm,
        k_norm=k_norm, q_norm=q_norm[0], rel_bias=rel_bias[0],
        w_kv=w_kv.astype(BF16), w_q=w_q[0].astype(BF16), w_o=w_o[0].astype(BF16),
        wk=[_fold_keys(peer_keys[i], peer_wq[i]) for i in range(peer_wq.shape[0])],
        u=peer_u.astype(BF16), v_t=jnp.swapaxes(peer_v, 1, 2).astype(BF16))


def kernel(x_prompt, x_sample, state_pool, cache_k, cache_v, norm_mix, norm_ffn, pool_w, pool_scale, kv_norm, w_kv, k_norm, w_q, q_norm, rel_bias, w_o, peer_wq, peer_keys, peer_u, peer_v):
    p = _prepare(norm_mix, norm_ffn, pool_w, pool_scale, kv_norm, w_kv, k_norm, w_q, q_norm, rel_bias, w_o,
                 peer_wq, peer_keys, peer_u, peer_v)
    hist0 = jnp.zeros((x_prompt.shape[0], POOL_HIST, D_MODEL), x_prompt.dtype)
    y_p, pool_p, k_p, v_p = _trunk(x_prompt, hist0, None, None, 0, p,
                                   tt=512, tb_sel=256, tb_dense=512, tb_qkv=512, qb=512)
    past_len = 4096
    y_s, pool_s, k_s, v_s = _trunk(x_sample, state_pool[0], cache_k, cache_v, past_len, p,
                                   tt=x_sample.shape[1], tb_sel=128, tb_dense=128, tb_qkv=128, qb=None)
    keep = min(BAND_PAST, x_prompt.shape[1])
    return (y_p, y_s, pool_p, pool_s, k_p[:, -keep:], v_p[:, -keep:], k_s, v_s)
```

```python
import functools

import numpy as np
import jax
import jax.numpy as jnp
from jax import lax
from jax.experimental import pallas as pl
from jax.experimental.pallas import tpu as pltpu

D_MODEL = 1024
CHUNK = 64
POOL_WINDOWS = (2, 4, 8, 16)
POOL_GROUP = D_MODEL // len(POOL_WINDOWS)
POOL_HIST = max(POOL_WINDOWS) - 1
HALO = POOL_HIST + 1
N_HEADS = 16
HEAD_DIM = 64
PAST_CHUNKS = 8
BAND_PAST = PAST_CHUNKS * CHUNK
BAND = BAND_PAST + CHUNK
REL_CLIP = 256
PEER_HEADS = 8
PEER_NKEYS = 128
PEER_N = PEER_NKEYS * PEER_NKEYS
PEER_HALF = 128
PEER_TOPK = 16
EPS = 1e-6
NOT_RANKED = 127.0
NEG_INF = float("-inf")
MASKED = -1e30

VMEM_LIMIT_BYTES = 56 * 1024 * 1024

F32 = jnp.float32
BF16 = jnp.bfloat16
NT_DIMS = (((1,), (1,)), ((), ()))


def _rms(x, g):
    return x * lax.rsqrt(jnp.mean(x * x, axis=-1, keepdims=True) + EPS) * g


def _params(*sem):
    return pltpu.CompilerParams(dimension_semantics=sem, vmem_limit_bytes=VMEM_LIMIT_BYTES)


def _pool_kernel(pos0, tt, n_t, x_ref, xp_ref, h_ref, g_ref, w_ref, sc_ref, o_ref, st_ref):
    i = pl.program_id(1)
    g = g_ref[...]
    x = x_ref[0]
    xn = _rms(x, g)
    halo = jnp.where(i == 0, h_ref[0], _rms(xp_ref[0], g))
    buf = jnp.concatenate([halo, xn], axis=0)
    pos = pos0 + i * tt + lax.broadcasted_iota(jnp.int32, (tt, 1), 0)
    outs = []
    for gi, wnd in enumerate(POOL_WINDOWS):
        sl = slice(gi * POOL_GROUP, (gi + 1) * POOL_GROUP)
        s = buf[:, sl]
        sh = 1
        while sh < wnd:
            s = s + pltpu.roll(s, sh, axis=0)
            sh *= 2
        cnt = jnp.minimum(pos + 1, wnd).astype(F32)
        pooled = s[HALO:] / cnt - xn[:, sl]
        outs.append(jnp.dot(pooled, w_ref[gi], precision=lax.Precision.HIGHEST,
                            preferred_element_type=F32))
    mix = jnp.concatenate(outs, axis=-1) * sc_ref[...]
    o_ref[0] = x + mix

    @pl.when(i == n_t - 1)
    def _():
        st_ref[0] = buf[tt:]


def _pool_mixer(x, hist, pos0, g, w, scale, tt):
    bsz, t, _ = x.shape
    n_t = t // tt
    hist16 = jnp.pad(hist, ((0, 0), (HALO - POOL_HIST, 0), (0, 0)))
    per_tt = tt // HALO
    out, st = pl.pallas_call(
        functools.partial(_pool_kernel, pos0, tt, n_t),
        grid=(bsz, n_t),
        in_specs=[
            pl.BlockSpec((1, tt, D_MODEL), lambda b, i: (b, i, 0)),
            pl.BlockSpec((1, HALO, D_MODEL), lambda b, i: (b, jnp.maximum(i * per_tt - 1, 0), 0)),
            pl.BlockSpec((1, HALO, D_MODEL), lambda b, i: (b, 0, 0)),
            pl.BlockSpec((1, D_MODEL), lambda b, i: (0, 0)),
            pl.BlockSpec((len(POOL_WINDOWS), POOL_GROUP, POOL_GROUP), lambda b, i: (0, 0, 0)),
            pl.BlockSpec((1, D_MODEL), lambda b, i: (0, 0)),
        ],
        out_specs=[
            pl.BlockSpec((1, tt, D_MODEL), lambda b, i: (b, i, 0)),
            pl.BlockSpec((1, HALO, D_MODEL), lambda b, i: (b, 0, 0)),
        ],
        out_shape=[
            jax.ShapeDtypeStruct((bsz, t, D_MODEL), F32),
            jax.ShapeDtypeStruct((bsz, HALO, D_MODEL), F32),
        ],
        compiler_params=_params("arbitrary", "arbitrary"),
        name="pool_mixer",
    )(x, x, hist16, g.reshape(1, D_MODEL), w, scale.reshape(1, D_MODEL))
    return out, st[:, HALO - POOL_HIST:]


def _top_values(s, want_rank):
    rank = jnp.full(s.shape, NOT_RANKED, F32) if want_rank else None
    vals = []
    for k in range(PEER_TOPK):
        m = jnp.max(s, axis=0, keepdims=True)
        hit = s == m
        if want_rank:
            rank = jnp.where(hit, float(k), rank)
        s = jnp.where(hit, NEG_INF, s)
        vals.append(m)
    return vals, rank


def _rows_to_block(rows):
    n = len(rows)
    rid = lax.broadcasted_iota(jnp.int32, (n, rows[0].shape[1]), 0)
    blk = jnp.zeros((n, rows[0].shape[1]), F32)
    for k, r in enumerate(rows):
        blk = jnp.where(rid == k, r, blk)
    return blk


def _bf16_pair(x):
    hi = pltpu.bitcast(x.astype(BF16).astype(F32), jnp.uint32)
    return hi | (hi >> 16)


def _fold_kernel(keys_ref, wq_ref, o_ref):
    o_ref[...] = lax.dot_general(keys_ref[0, 0], wq_ref[...], NT_DIMS, precision=lax.Precision.HIGHEST,
                                 preferred_element_type=F32).astype(BF16)


def _fold_keys(keys, wq):
    return pl.pallas_call(
        _fold_kernel,
        grid=(PEER_HEADS, 2),
        in_specs=[pl.BlockSpec((1, 1, PEER_NKEYS, PEER_HALF), lambda h, p: (h, p, 0, 0)),
                  pl.BlockSpec((D_MODEL, PEER_HALF), lambda h, p: (0, 2 * h + p))],
        out_specs=pl.BlockSpec((PEER_NKEYS, D_MODEL), lambda h, p: (2 * h + p, 0)),
        out_shape=jax.ShapeDtypeStruct((PEER_HEADS * 2 * PEER_NKEYS, D_MODEL), BF16),
        compiler_params=_params("arbitrary", "arbitrary"),
        name="fold_keys",
    )(keys, wq)


def _select_head(s, h, r2_ref, l_ref, c_ref, e2_ref):
    s1, s2 = s[:PEER_NKEYS], s[PEER_NKEYS:]
    v1, _ = _top_values(s1, False)
    v2, rank2 = _top_values(s2, True)
    v2_blk = _rows_to_block(v2)
    rid = lax.broadcasted_iota(jnp.int32, (F32_ROWS, v2_blk.shape[1]), 0)
    cands = [v1[0] + v2_blk]
    for r1 in range(1, F32_ROWS):
        cands.append(jnp.where(rid < PEER_TOPK // (r1 + 1), v1[r1] + v2_blk[:F32_ROWS], NEG_INF))
    assert PEER_TOPK // (F32_ROWS + 1) == 1
    tail = _rows_to_block(v1[F32_ROWS:]) + v2[0]
    x = jnp.concatenate(cands + [tail], axis=0)
    tops = []
    for _ in range(PEER_TOPK):
        m = jnp.max(x, axis=0, keepdims=True)
        x = jnp.where(x == m, NEG_INF, x)
        tops.append(m)
    tau = tops[-1]
    z = jnp.zeros_like(tau)
    for m in tops:
        z = z + jnp.exp(m - tops[0])
    l_of_i1 = jnp.zeros(s1.shape, F32)
    for r1 in range(PEER_TOPK):
        if r1 < F32_ROWS:
            n_sel = jnp.sum((cands[r1] >= tau).astype(F32), axis=0, keepdims=True)
        else:
            n_sel = ((v1[r1] + v2[0]) >= tau).astype(F32)
        l_of_i1 = jnp.where(s1 == v1[r1], n_sel, l_of_i1)
    r2_ref[h] = rank2.astype(BF16)
    l_ref[h] = _bf16_pair(l_of_i1)
    c_ref[h] = _bf16_pair(jnp.exp(s1 - v1[0]) / z)
    e2_ref[h] = jnp.exp(s2 - v2[0]).astype(BF16)


def _select_kernel(x_ref, g_ref, wk_ref, xn_ref, r2_ref, l_ref, c_ref, e2_ref):
    xn = _rms(x_ref[...], g_ref[...]).astype(BF16)
    xn_ref[...] = xn

    def scores(h):
        return lax.dot_general(wk_ref[h * 2 * PEER_NKEYS:(h + 1) * 2 * PEER_NKEYS, :], xn, NT_DIMS,
                               preferred_element_type=F32)

    s_next = scores(0)
    for h in range(PEER_HEADS):
        s = s_next
        if h + 1 < PEER_HEADS:
            s_next = scores(h + 1)
        _select_head(s, h, r2_ref, l_ref, c_ref, e2_ref)


def _peer_select(x, g, wk, tb):
    t = x.shape[0]
    pair = jax.ShapeDtypeStruct((PEER_HEADS, PEER_NKEYS, t), jnp.uint32)
    feat16 = jax.ShapeDtypeStruct((PEER_HEADS, PEER_NKEYS, t), BF16)
    feat_spec = pl.BlockSpec((PEER_HEADS, PEER_NKEYS, tb), lambda i: (0, 0, i))
    return pl.pallas_call(
        _select_kernel,
        grid=(t // tb,),
        in_specs=[
            pl.BlockSpec((tb, D_MODEL), lambda i: (i, 0)),
            pl.BlockSpec((1, D_MODEL), lambda i: (0, 0)),
            pl.BlockSpec((PEER_HEADS * 2 * PEER_NKEYS, D_MODEL), lambda i: (0, 0)),
        ],
        out_specs=[pl.BlockSpec((tb, D_MODEL), lambda i: (i, 0)),
                   feat_spec, feat_spec, feat_spec, feat_spec],
        out_shape=[jax.ShapeDtypeStruct((t, D_MODEL), BF16), feat16, pair, pair, feat16],
        compiler_params=_params("arbitrary"),
        name="peer_select",
    )(x, g.reshape(1, D_MODEL), wk)


I1_PER_CHUNK = 16
E_CHUNK = I1_PER_CHUNK * PEER_NKEYS


BF16_ROWS = 16
F32_ROWS = 8


def _packed_rows(word_row):
    return pltpu.bitcast(jnp.broadcast_to(word_row, (F32_ROWS, word_row.shape[1])), BF16)


DENSE_PARTS = 4


def _dense_kernel(n_chunks, x_ref, xn_ref, u_ref, vt_ref, r2_ref, e2_ref, l_ref, c_ref, o_ref,
                  a_ref, acc_ref):
    s = pl.program_id(0)

    @pl.when(s == 0)
    def _():
        a_ref[...] = jnp.zeros_like(a_ref)
        acc_ref[...] = jnp.zeros_like(acc_ref)

    chunk = jnp.maximum(s - 1, 0) % n_chunks
    a_cur = a_ref[...]
    tb = a_cur.shape[1]

    def pre_activations(half):
        tk = slice(half * (tb // 2), (half + 1) * (tb // 2))
        a_ref[:, tk] = lax.dot_general(u_ref[...], xn_ref[tk, :], NT_DIMS,
                                       preferred_element_type=F32).astype(BF16)

    i1_per_part = I1_PER_CHUNK // DENSE_PARTS
    part_rows = i1_per_part * PEER_NKEYS
    total = None
    for p in range(DENSE_PARTS):
        pieces = []
        for j in range(p * i1_per_part, (p + 1) * i1_per_part):
            rows = [(_packed_rows(l_ref[h, j:j + 1, :]), _packed_rows(c_ref[h, j:j + 1, :]))
                    for h in range(PEER_HEADS)]
            for b in range(PEER_NKEYS // BF16_ROWS):
                i2 = slice(b * BF16_ROWS, (b + 1) * BF16_ROWS)
                gate = None
                for h in range(PEER_HEADS):
                    l_rows, c_rows = rows[h]
                    term = jnp.where(r2_ref[h, i2, :] < l_rows, e2_ref[h, i2, :] * c_rows,
                                     jnp.zeros((), BF16))
                    gate = term if gate is None else gate + term
                r0 = j * PEER_NKEYS + b * BF16_ROWS
                a_b = a_cur[r0:r0 + BF16_ROWS]
                pieces.append(0.5 * a_b * (1.0 + lax.erf(a_b * (2.0 ** -0.5))) * gate)
        h_t = jnp.concatenate(pieces, axis=0)
        part = jnp.dot(vt_ref[:, p * part_rows:(p + 1) * part_rows], h_t, preferred_element_type=F32)
        total = part if total is None else total + part
        if p % (DENSE_PARTS // 2) == 0:
            pre_activations(p // (DENSE_PARTS // 2))
    acc_ref[...] += total

    @pl.when(jnp.logical_and(chunk == n_chunks - 1, s >= 1))
    def _():
        o_ref[...] = x_ref[...] + acc_ref[...].T
        acc_ref[...] = jnp.zeros_like(acc_ref)


def _peer_dense(x, xn, u, v_t, layer, r2, e2, l, cc, tb):
    t = x.shape[0]
    n_chunks = PEER_N // E_CHUNK
    n_steps = (t // tb) * n_chunks

    def ahead(s):
        k = jnp.minimum(s, n_steps - 1)
        return k // n_chunks, k % n_chunks

    def gated(s):
        k = jnp.maximum(s - 1, 0)
        return k // n_chunks, k % n_chunks

    tok = pl.BlockSpec((tb, D_MODEL), lambda s: (gated(s)[0], 0))
    full = pl.BlockSpec((PEER_HEADS, PEER_NKEYS, tb), lambda s: (0, 0, gated(s)[0]))
    part = pl.BlockSpec((PEER_HEADS, I1_PER_CHUNK, tb), lambda s: (0, gated(s)[1], gated(s)[0]))
    return pl.pallas_call(
        functools.partial(_dense_kernel, n_chunks),
        grid=(n_steps + 1,),
        in_specs=[tok,
                  pl.BlockSpec((tb, D_MODEL), lambda s: (ahead(s)[0], 0)),
                  pl.BlockSpec((None, E_CHUNK, D_MODEL), lambda s: (layer, ahead(s)[1], 0)),
                  pl.BlockSpec((None, D_MODEL, E_CHUNK), lambda s: (layer, 0, gated(s)[1])),
                  full, full, part, part],
        out_specs=tok,
        out_shape=jax.ShapeDtypeStruct((t, D_MODEL), F32),
        scratch_shapes=[pltpu.VMEM((E_CHUNK, tb), BF16), pltpu.VMEM((D_MODEL, tb), F32)],
        compiler_params=_params("arbitrary"),
        name="peer_dense",
    )(x, xn, u, v_t, r2, e2, l, cc)


def _peer(x, g, wk, u, v_t, layer, tb_sel, tb_dense):
    xn, r2, l, cc, e2 = _peer_select(x, g, wk, tb_sel)
    return _peer_dense(x, xn, u, v_t, layer, r2, e2, l, cc, tb_dense)


def _head_norm(z, head_mean, gn):
    zz = z * z
    hi = zz.astype(BF16)
    lo = (zz - hi.astype(F32)).astype(BF16)
    ms = (jnp.dot(hi, head_mean, preferred_element_type=F32)
          + jnp.dot(lo, head_mean, preferred_element_type=F32))
    return z * lax.rsqrt(ms + EPS) * gn


def _qkv_kernel(x_ref, gq_ref, gkv_ref, wq_ref, wkv_ref, hm_ref, qn_ref, kn_ref, q_ref, k_ref, v_ref):
    x = x_ref[...]
    hm = hm_ref[...]
    q = jnp.dot(_rms(x, gq_ref[...]).astype(BF16), wq_ref[...], preferred_element_type=F32)
    kv = jnp.dot(_rms(x, gkv_ref[...]).astype(BF16), wkv_ref[...], preferred_element_type=F32)
    q_ref[...] = _head_norm(q, hm, qn_ref[...])
    k_ref[...] = _head_norm(kv[:, :D_MODEL], hm, kn_ref[...])
    v_ref[...] = kv[:, D_MODEL:]


def _qkv(x, gq, gkv, wq, wkv, qn, kn, tb):
    t = x.shape[0]
    head_id = np.arange(D_MODEL) // HEAD_DIM
    head_mean = jnp.asarray((head_id[:, None] == head_id[None, :]) / HEAD_DIM, BF16)
    tok = pl.BlockSpec((tb, D_MODEL), lambda i: (i, 0))
    vec = pl.BlockSpec((1, D_MODEL), lambda i: (0, 0))
    sq = pl.BlockSpec((D_MODEL, D_MODEL), lambda i: (0, 0))
    out = jax.ShapeDtypeStruct((t, D_MODEL), F32)
    return pl.pallas_call(
        _qkv_kernel,
        grid=(t // tb,),
        in_specs=[tok, vec, vec, sq, pl.BlockSpec((D_MODEL, 2 * D_MODEL), lambda i: (0, 0)), sq, vec, vec],
        out_specs=[tok, tok, tok],
        out_shape=[out, out, out],
        compiler_params=_params("arbitrary"),
        name="qkv_proj",
    )(x, gq.reshape(1, D_MODEL), gkv.reshape(1, D_MODEL), wq, wkv, head_mean,
      jnp.tile(qn, N_HEADS).reshape(1, D_MODEL), jnp.tile(kn, N_HEADS).reshape(1, D_MODEL))


def _attend_rows(q_rows, k_band, v_band, bias_ref, valid):
    r = q_rows.shape[0]
    lane = lax.broadcasted_iota(jnp.int32, (1, 2 * HEAD_DIM), 1)
    first = lane < HEAD_DIM
    pairs = [slice(hp * 2 * HEAD_DIM, (hp + 1) * 2 * HEAD_DIM) for hp in range(N_HEADS // 2)]

    def scores(hp):
        q2 = q_rows[:, pairs[hp]]
        qm = jnp.concatenate([jnp.where(first, q2, 0.0), jnp.where(first, 0.0, q2)], axis=0).astype(BF16)
        s = lax.dot_general(qm, k_band[:, pairs[hp]], NT_DIMS, preferred_element_type=F32)
        s = s * (HEAD_DIM ** -0.5) + bias_ref[hp]
        return s if valid is None else jnp.where(valid, s, MASKED)

    def attend(hp, s):
        p = jnp.exp(s - jnp.max(s, axis=-1, keepdims=True))
        den = jnp.sum(p, axis=-1, keepdims=True)
        o2 = jnp.dot(p.astype(BF16), v_band[:, pairs[hp]], preferred_element_type=F32) / den
        return jnp.where(first, o2[:r], o2[r:])

    outs = []
    s_next = scores(0)
    for hp in range(len(pairs)):
        s_cur = s_next
        if hp + 1 < len(pairs):
            s_next = scores(hp + 1)
        outs.append(attend(hp, s_cur))
    return jnp.concatenate(outs, axis=-1)


def _attn_prompt_kernel(qb, q_ref, kp_ref, kc_ref, vp_ref, vc_ref, bias_ref, wo_ref, x_ref, o_ref,
                        kcat, vcat, oscr):
    i = pl.program_id(0)
    kcat[:qb] = kp_ref[...].astype(BF16)
    kcat[qb:] = kc_ref[...].astype(BF16)
    vcat[:qb] = vp_ref[...].astype(BF16)
    vcat[qb:] = vc_ref[...].astype(BF16)
    col = lax.broadcasted_iota(jnp.int32, (1, BAND), 1)

    def chunk(j, carry):
        r0 = pl.multiple_of(j * CHUNK, CHUNK)
        k0 = pl.multiple_of(qb - BAND_PAST + j * CHUNK, CHUNK)
        valid = (i * qb + j * CHUNK - BAND_PAST + col) >= 0
        o = _attend_rows(q_ref[pl.ds(r0, CHUNK), :], kcat[pl.ds(k0, BAND), :], vcat[pl.ds(k0, BAND), :],
                         bias_ref, valid)
        oscr[pl.ds(r0, CHUNK), :] = o.astype(BF16)
        return carry

    lax.fori_loop(0, qb // CHUNK, chunk, 0)
    o_ref[...] = x_ref[...] + jnp.dot(oscr[...], wo_ref[...], preferred_element_type=F32)


def _attn_prompt(x, q, k, v, bias, wo, qb):
    t = x.shape[0]
    cur = pl.BlockSpec((qb, D_MODEL), lambda i: (i, 0))
    prev = pl.BlockSpec((qb, D_MODEL), lambda i: (jnp.maximum(i - 1, 0), 0))
    return pl.pallas_call(
        functools.partial(_attn_prompt_kernel, qb),
        grid=(t // qb,),
        in_specs=[cur, prev, cur, prev, cur,
                  pl.BlockSpec((N_HEADS // 2, 2 * CHUNK, BAND), lambda i: (0, 0, 0)),
                  pl.BlockSpec((D_MODEL, D_MODEL), lambda i: (0, 0)),
                  cur],
        out_specs=cur,
        out_shape=jax.ShapeDtypeStruct((t, D_MODEL), F32),
        scratch_shapes=[pltpu.VMEM((2 * qb, D_MODEL), BF16), pltpu.VMEM((2 * qb, D_MODEL), BF16),
                        pltpu.VMEM((qb, D_MODEL), BF16)],
        compiler_params=_params("arbitrary"),
        name="attn_prompt",
    )(q, k, k, v, v, bias, wo, x)


def _attn_sample_kernel(rows, q_ref, ck_ref, kn_ref, cv_ref, vn_ref, bias_ref, wo_ref, x_ref, o_ref,
                        kcat, vcat):
    kcat[:rows] = ck_ref[0].astype(BF16)
    kcat[rows:] = kn_ref[0].astype(BF16)
    vcat[:rows] = cv_ref[0].astype(BF16)
    vcat[rows:] = vn_ref[0].astype(BF16)
    o = _attend_rows(q_ref[0], kcat[...], vcat[...], bias_ref, None)
    o_ref[0] = x_ref[0] + jnp.dot(o.astype(BF16), wo_ref[...], preferred_element_type=F32)


def _attn_sample(x, q, k, v, cache_k, cache_v, bias, wo):
    bsz, t, _ = x.shape
    rows = cache_k.shape[1]
    new = pl.BlockSpec((1, t, D_MODEL), lambda b: (b, 0, 0))
    old = pl.BlockSpec((1, rows, D_MODEL), lambda b: (b, 0, 0))
    return pl.pallas_call(
        functools.partial(_attn_sample_kernel, rows),
        grid=(bsz,),
        in_specs=[new, old, new, old, new,
                  pl.BlockSpec((N_HEADS // 2, 2 * t, rows + t), lambda b: (0, 0, 0)),
                  pl.BlockSpec((D_MODEL, D_MODEL), lambda b: (0, 0)),
                  new],
        out_specs=new,
        out_shape=jax.ShapeDtypeStruct((bsz, t, D_MODEL), F32),
        scratch_shapes=[pltpu.VMEM((rows + t, D_MODEL), BF16), pltpu.VMEM((rows + t, D_MODEL), BF16)],
        compiler_params=_params("arbitrary"),
        name="attn_sample",
    )(q, cache_k, k, cache_v, v, bias, wo, x)


def _band_bias(rel_bias, n_q, n_k):
    m = np.arange(n_q - 1 + n_k)
    diag = rel_bias[:, np.clip(BAND_PAST + n_q - 1 - m, -REL_CLIP, REL_CLIP) + REL_CLIP]
    bias = jnp.stack([diag[:, n_q - 1 - qo:n_q - 1 - qo + n_k] for qo in range(n_q)], axis=1)
    return bias.reshape(N_HEADS // 2, 2 * n_q, n_k)


def _trunk(x, hist, cache_k, cache_v, pos0, p, tt, tb_sel, tb_dense, tb_qkv, qb):
    bsz, t, _ = x.shape
    n = bsz * t
    x1, st = _pool_mixer(x, hist, pos0, p["norm_mix"][0], p["pool_w"][0], p["pool_scale"][0], tt)
    x1 = x1.reshape(n, D_MODEL)
    x2 = _peer(x1, p["norm_ffn"][0], p["wk"][0], p["u"], p["v_t"], 0,
               tb_sel, tb_dense)
    q, k, v = _qkv(x2, p["norm_mix"][1], p["kv_norm"], p["w_q"], p["w_kv"], p["q_norm"], p["k_norm"], tb_qkv)
    if cache_k is None:
        assert bsz == 1 and t % qb == 0 and qb >= BAND_PAST and pos0 == 0
        x3 = _attn_prompt(x2, q, k, v, _band_bias(p["rel_bias"], CHUNK, BAND), p["w_o"], qb)
    else:
        rows = cache_k.shape[1]
        assert pos0 % CHUNK == 0 and t <= CHUNK and rows == BAND_PAST and pos0 >= rows
        bias = _band_bias(p["rel_bias"], t, rows + t)
        x3 = _attn_sample(x2.reshape(bsz, t, D_MODEL), q.reshape(bsz, t, D_MODEL),
                          k.reshape(bsz, t, D_MODEL), v.reshape(bsz, t, D_MODEL),
                          cache_k.reshape(bsz, rows, D_MODEL), cache_v.reshape(bsz, rows, D_MODEL),
                          bias, p["w_o"]).reshape(n, D_MODEL)
    x4 = _peer(x3, p["norm_ffn"][1], p["wk"][1], p["u"], p["v_t"], 1,
               tb_sel, tb_dense)
    return (x4.reshape(bsz, t, D_MODEL), st[None],
            k.reshape(bsz, t, N_HEADS, HEAD_DIM), v.reshape(bsz, t, N_HEADS, HEAD_DIM))


def _prepare(norm_mix, norm_ffn, pool_w, pool_scale, kv_norm, w_kv, k_norm, w_q, q_norm, rel_bias, w_o,
             peer_wq, peer_keys, peer_u, peer_v):
    assert w_q.shape[0] == 1 and pool_w.shape[0] == 1
    return dict(
        norm_mix=norm_mix, norm_ffn=norm_ffn, pool_w=pool_w, pool_scale=pool_scale, kv_norm=kv_norm,
        k_norm=k_norm, q_norm=q_norm[0], rel_bias=rel_bias[0],
        w_kv=w_kv.astype(BF16), w_q=w_q[0].astype(BF16), w_o=w_o[0].astype(BF16),
        wk=[_fold_keys(peer_keys[i], peer_wq[i]) for i in range(peer_wq.shape[0])],
        u=peer_u.astype(BF16), v_t=jnp.swapaxes(peer_v, 1, 2).astype(BF16))


def kernel(x_prompt, x_sample, state_pool, cache_k, cache_v, norm_mix, norm_ffn, pool_w, pool_scale, kv_norm, w_kv, k_norm, w_q, q_norm, rel_bias, w_o, peer_wq, peer_keys, peer_u, peer_v):
    p = _prepare(norm_mix, norm_ffn, pool_w, pool_scale, kv_norm, w_kv, k_norm, w_q, q_norm, rel_bias, w_o,
                 peer_wq, peer_keys, peer_u, peer_v)
    hist0 = jnp.zeros((x_prompt.shape[0], POOL_HIST, D_MODEL), x_prompt.dtype)
    y_p, pool_p, k_p, v_p = _trunk(x_prompt, hist0, None, None, 0, p,
                                   tt=512, tb_sel=256, tb_dense=512, tb_qkv=512, qb=512)
    past_len = 4096
    y_s, pool_s, k_s, v_s = _trunk(x_sample, state_pool[0], cache_k, cache_v, past_len, p,
                                   tt=x_sample.shape[1], tb_sel=128, tb_dense=128, tb_qkv=128, qb=None)
    keep = min(BAND_PAST, x_prompt.shape[1])
    return (y_p, y_s, pool_p, pool_s, k_p[:, -keep:], v_p[:, -keep:], k_s, v_s)
```

```python
import functools

import numpy as np
import jax
import jax.numpy as jnp
from jax import lax
from jax.experimental import pallas as pl
from jax.experimental.pallas import tpu as pltpu

D_MODEL = 1024
CHUNK = 64
POOL_WINDOWS = (2, 4, 8, 16)
POOL_GROUP = D_MODEL // len(POOL_WINDOWS)
POOL_HIST = max(POOL_WINDOWS) - 1
HALO = POOL_HIST + 1
N_HEADS = 16
HEAD_DIM = 64
PAST_CHUNKS = 8
BAND_PAST = PAST_CHUNKS * CHUNK
BAND = BAND_PAST + CHUNK
REL_CLIP = 256
PEER_HEADS = 8
PEER_NKEYS = 128
PEER_N = PEER_NKEYS * PEER_NKEYS
PEER_HALF = 128
PEER_TOPK = 16
EPS = 1e-6
NOT_RANKED = 127.0
NEG_INF = float("-inf")
MASKED = -1e30

VMEM_LIMIT_BYTES = 56 * 1024 * 1024

F32 = jnp.float32
BF16 = jnp.bfloat16
NT_DIMS = (((1,), (1,)), ((), ()))


def _rms(x, g):
    return x * lax.rsqrt(jnp.mean(x * x, axis=-1, keepdims=True) + EPS) * g


def _params(*sem):
    return pltpu.CompilerParams(dimension_semantics=sem, vmem_limit_bytes=VMEM_LIMIT_BYTES)


def _pool_kernel(pos0, tt, n_t, x_ref, xp_ref, h_ref, g_ref, w_ref, sc_ref, o_ref, st_ref):
    i = pl.program_id(1)
    g = g_ref[...]
    x = x_ref[0]
    xn = _rms(x, g)
    halo = jnp.where(i == 0, h_ref[0], _rms(xp_ref[0], g))
    buf = jnp.concatenate([halo, xn], axis=0)
    pos = pos0 + i * tt + lax.broadcasted_iota(jnp.int32, (tt, 1), 0)
    outs = []
    for gi, wnd in enumerate(POOL_WINDOWS):
        sl = slice(gi * POOL_GROUP, (gi + 1) * POOL_GROUP)
        s = buf[:, sl]
        sh = 1
        while sh < wnd:
            s = s + pltpu.roll(s, sh, axis=0)
            sh *= 2
        cnt = jnp.minimum(pos + 1, wnd).astype(F32)
        pooled = s[HALO:] / cnt - xn[:, sl]
        outs.append(jnp.dot(pooled, w_ref[gi], precision=lax.Precision.HIGHEST,
                            preferred_element_type=F32))
    mix = jnp.concatenate(outs, axis=-1) * sc_ref[...]
    o_ref[0] = x + mix

    @pl.when(i == n_t - 1)
    def _():
        st_ref[0] = buf[tt:]


def _pool_mixer(x, hist, pos0, g, w, scale, tt):
    bsz, t, _ = x.shape
    n_t = t // tt
    hist16 = jnp.pad(hist, ((0, 0), (HALO - POOL_HIST, 0), (0, 0)))
    per_tt = tt // HALO
    out, st = pl.pallas_call(
        functools.partial(_pool_kernel, pos0, tt, n_t),
        grid=(bsz, n_t),
        in_specs=[
            pl.BlockSpec((1, tt, D_MODEL), lambda b, i: (b, i, 0)),
            pl.BlockSpec((1, HALO, D_MODEL), lambda b, i: (b, jnp.maximum(i * per_tt - 1, 0), 0)),
            pl.BlockSpec((1, HALO, D_MODEL), lambda b, i: (b, 0, 0)),
            pl.BlockSpec((1, D_MODEL), lambda b, i: (0, 0)),
            pl.BlockSpec((len(POOL_WINDOWS), POOL_GROUP, POOL_GROUP), lambda b, i: (0, 0, 0)),
            pl.BlockSpec((1, D_MODEL), lambda b, i: (0, 0)),
        ],
        out_specs=[
            pl.BlockSpec((1, tt, D_MODEL), lambda b, i: (b, i, 0)),
            pl.BlockSpec((1, HALO, D_MODEL), lambda b, i: (b, 0, 0)),
        ],
        out_shape=[
            jax.ShapeDtypeStruct((bsz, t, D_MODEL), F32),
            jax.ShapeDtypeStruct((bsz, HALO, D_MODEL), F32),
        ],
        compiler_params=_params("arbitrary", "arbitrary"),
        name="pool_mixer",
    )(x, x, hist16, g.reshape(1, D_MODEL), w, scale.reshape(1, D_MODEL))
    return out, st[:, HALO - POOL_HIST:]


def _top_values(s, want_rank):
    rank = jnp.full(s.shape, NOT_RANKED, F32) if want_rank else None
    vals = []
    for k in range(PEER_TOPK):
        m = jnp.max(s, axis=0, keepdims=True)
        hit = s == m
        if want_rank:
            rank = jnp.where(hit, float(k), rank)
        s = jnp.where(hit, NEG_INF, s)
        vals.append(m)
    return vals, rank


def _rows_to_block(rows):
    n = len(rows)
    rid = lax.broadcasted_iota(jnp.int32, (n, rows[0].shape[1]), 0)
    blk = jnp.zeros((n, rows[0].shape[1]), F32)
    for k, r in enumerate(rows):
        blk = jnp.where(rid == k, r, blk)
    return blk


def _fold_kernel(keys_ref, wq_ref, o_ref):
    o_ref[...] = lax.dot_general(keys_ref[0, 0], wq_ref[...], NT_DIMS, precision=lax.Precision.HIGHEST,
                                 preferred_element_type=F32).astype(BF16)


def _fold_keys(keys, wq):
    return pl.pallas_call(
        _fold_kernel,
        grid=(PEER_HEADS, 2),
        in_specs=[pl.BlockSpec((1, 1, PEER_NKEYS, PEER_HALF), lambda h, p: (h, p, 0, 0)),
                  pl.BlockSpec((D_MODEL, PEER_HALF), lambda h, p: (0, 2 * h + p))],
        out_specs=pl.BlockSpec((PEER_NKEYS, D_MODEL), lambda h, p: (2 * h + p, 0)),
        out_shape=jax.ShapeDtypeStruct((PEER_HEADS * 2 * PEER_NKEYS, D_MODEL), BF16),
        compiler_params=_params("arbitrary", "arbitrary"),
        name="fold_keys",
    )(keys, wq)


def _select_head(s, h, r2_ref, l_ref, c_ref, e2_ref):
    s1, s2 = s[:PEER_NKEYS], s[PEER_NKEYS:]
    v1, _ = _top_values(s1, False)
    v2, rank2 = _top_values(s2, True)
    v2_blk = _rows_to_block(v2)
    rid = lax.broadcasted_iota(jnp.int32, (F32_ROWS, v2_blk.shape[1]), 0)
    cands = [v1[0] + v2_blk]
    for r1 in range(1, F32_ROWS):
        cands.append(jnp.where(rid < PEER_TOPK // (r1 + 1), v1[r1] + v2_blk[:F32_ROWS], NEG_INF))
    assert PEER_TOPK // (F32_ROWS + 1) == 1
    tail = _rows_to_block(v1[F32_ROWS:]) + v2[0]
    x0 = jnp.concatenate(cands + [tail], axis=0)
    x = x0
    tops = []
    for _ in range(PEER_TOPK):
        m = jnp.max(x, axis=0, keepdims=True)
        x = jnp.where(x == m, NEG_INF, x)
        tops.append(m)

    def count(op, thr):
        return [jnp.sum(op(cands[r1], thr).astype(F32), axis=0, keepdims=True) if r1 < F32_ROWS
                else op(v1[r1] + v2[0], thr).astype(F32) for r1 in range(PEER_TOPK)]

    def no_ties():
        z = jnp.zeros_like(tops[0])
        for m in tops:
            z = z + jnp.exp(m - tops[0])
        return [z] + count(jnp.greater_equal, tops[-1])

    def with_ties():
        tau = tops[-1]
        for k in range(PEER_TOPK - 2, -1, -1):
            at_least = jnp.sum((x0 >= tops[k]).astype(F32), axis=0, keepdims=True)
            tau = jnp.where(at_least >= PEER_TOPK, tops[k], tau)
        above, equal = count(jnp.greater, tau), count(jnp.equal, tau)
        budget = float(PEER_TOPK) - functools.reduce(jnp.add, above)
        z = (jnp.sum(jnp.where(x0 > tau, jnp.exp(x0 - tops[0]), 0.0), axis=0, keepdims=True)
             + budget * jnp.exp(tau - tops[0]))
        n_sel, used = [], jnp.zeros_like(budget)
        for r1 in range(PEER_TOPK):
            n_sel.append(above[r1] + jnp.clip(budget - used, 0.0, equal[r1]))
            used = used + equal[r1]
        return [z] + n_sel

    def write_first_half(z, *n_sel):
        l_of_i1 = jnp.zeros(s1.shape, F32)
        for r1 in range(PEER_TOPK):
            l_of_i1 = jnp.where(s1 == v1[r1], n_sel[r1], l_of_i1)
        l_ref[h] = l_of_i1.astype(BF16)
        c_ref[h] = (jnp.exp(s1 - v1[0]) / z).astype(BF16)

    n_reached = jnp.sum((x0 >= tops[-1]).astype(F32), axis=0, keepdims=True)
    write_first_half(*no_ties())
    r2_ref[h] = rank2.astype(BF16)
    e2_ref[h] = jnp.exp(s2 - v2[0]).astype(BF16)

    @pl.when(jnp.max(n_reached) > PEER_TOPK)
    def _():
        write_first_half(*with_ties())


def _select_kernel(x_ref, g_ref, wk_ref, xn_ref, r2_ref, l_ref, c_ref, e2_ref):
    xn = _rms(x_ref[...], g_ref[...]).astype(BF16)
    xn_ref[...] = xn

    def scores(h):
        return lax.dot_general(wk_ref[h * 2 * PEER_NKEYS:(h + 1) * 2 * PEER_NKEYS, :], xn, NT_DIMS,
                               preferred_element_type=F32)

    s_next = scores(0)
    for h in range(PEER_HEADS):
        s = s_next
        if h + 1 < PEER_HEADS:
            s_next = scores(h + 1)
        _select_head(s, h, r2_ref, l_ref, c_ref, e2_ref)


def _peer_select(x, g, wk, tb):
    t = x.shape[0]
    feat16 = jax.ShapeDtypeStruct((PEER_HEADS, PEER_NKEYS, t), BF16)
    feat_spec = pl.BlockSpec((PEER_HEADS, PEER_NKEYS, tb), lambda i: (0, 0, i))
    return pl.pallas_call(
        _select_kernel,
        grid=(t // tb,),
        in_specs=[
            pl.BlockSpec((tb, D_MODEL), lambda i: (i, 0)),
            pl.BlockSpec((1, D_MODEL), lambda i: (0, 0)),
            pl.BlockSpec((PEER_HEADS * 2 * PEER_NKEYS, D_MODEL), lambda i: (0, 0)),
        ],
        out_specs=[pl.BlockSpec((tb, D_MODEL), lambda i: (i, 0)),
                   feat_spec, feat_spec, feat_spec, feat_spec],
        out_shape=[jax.ShapeDtypeStruct((t, D_MODEL), BF16), feat16, feat16, feat16, feat16],
        compiler_params=_params("arbitrary"),
        name="peer_select",
    )(x, g.reshape(1, D_MODEL), wk)


I1_PER_CHUNK = 16
E_CHUNK = I1_PER_CHUNK * PEER_NKEYS


BF16_ROWS = 16
F32_ROWS = 8


def _packed_rows(row):
    return jnp.broadcast_to(row, (BF16_ROWS, row.shape[1]))


DENSE_PARTS = 2


def _dense_kernel(n_chunks, x_ref, xn_ref, u_ref, vt_ref, r2_ref, e2_ref, l_ref, c_ref, o_ref,
                  a_ref, acc_ref):
    s = pl.program_id(0)

    @pl.when(s == 0)
    def _():
        a_ref[...] = jnp.zeros_like(a_ref)
        acc_ref[...] = jnp.zeros_like(acc_ref)

    chunk = jnp.maximum(s - 1, 0) % n_chunks
    a_cur = a_ref[...]
    tb = a_cur.shape[1]

    def pre_activations(half):
        tk = slice(half * (tb // 2), (half + 1) * (tb // 2))
        a_ref[:, tk] = lax.dot_general(u_ref[...], xn_ref[tk, :], NT_DIMS,
                                       preferred_element_type=F32).astype(BF16)

    i1_per_part = I1_PER_CHUNK // DENSE_PARTS
    part_rows = i1_per_part * PEER_NKEYS
    total = None
    for p in range(DENSE_PARTS):
        pieces = []
        for j in range(p * i1_per_part, (p + 1) * i1_per_part):
            rows = [(_packed_rows(l_ref[h, j:j + 1, :]), _packed_rows(c_ref[h, j:j + 1, :]))
                    for h in range(PEER_HEADS)]
            for b in range(PEER_NKEYS // BF16_ROWS):
                i2 = slice(b * BF16_ROWS, (b + 1) * BF16_ROWS)
                gate = None
                for h in range(PEER_HEADS):
                    l_rows, c_rows = rows[h]
                    term = jnp.where(r2_ref[h, i2, :] < l_rows, e2_ref[h, i2, :] * c_rows,
                                     jnp.zeros((), BF16))
                    gate = term if gate is None else gate + term
                r0 = j * PEER_NKEYS + b * BF16_ROWS
                a_b = a_cur[r0:r0 + BF16_ROWS]
                pieces.append(0.5 * a_b * (1.0 + lax.erf(a_b * (2.0 ** -0.5))) * gate)
        h_t = jnp.concatenate(pieces, axis=0)
        part = jnp.dot(vt_ref[:, p * part_rows:(p + 1) * part_rows], h_t, preferred_element_type=F32)
        total = part if total is None else total + part
        if p % (DENSE_PARTS // 2) == 0:
            pre_activations(p // (DENSE_PARTS // 2))
    acc_ref[...] += total

    @pl.when(jnp.logical_and(chunk == n_chunks - 1, s >= 1))
    def _():
        o_ref[...] = x_ref[...] + acc_ref[...].T
        acc_ref[...] = jnp.zeros_like(acc_ref)


def _peer_dense(x, xn, u, v_t, layer, r2, e2, l, cc, tb):
    t = x.shape[0]
    n_chunks = PEER_N // E_CHUNK
    n_steps = (t // tb) * n_chunks

    def ahead(s):
        k = jnp.minimum(s, n_steps - 1)
        return k // n_chunks, k % n_chunks

    def gated(s):
        k = jnp.maximum(s - 1, 0)
        return k // n_chunks, k % n_chunks

    tok = pl.BlockSpec((tb, D_MODEL), lambda s: (gated(s)[0], 0))
    full = pl.BlockSpec((PEER_HEADS, PEER_NKEYS, tb), lambda s: (0, 0, gated(s)[0]))
    part = pl.BlockSpec((PEER_HEADS, I1_PER_CHUNK, tb), lambda s: (0, gated(s)[1], gated(s)[0]))
    return pl.pallas_call(
        functools.partial(_dense_kernel, n_chunks),
        grid=(n_steps + 1,),
        in_specs=[tok,
                  pl.BlockSpec((tb, D_MODEL), lambda s: (ahead(s)[0], 0)),
                  pl.BlockSpec((None, E_CHUNK, D_MODEL), lambda s: (layer, ahead(s)[1], 0)),
                  pl.BlockSpec((None, D_MODEL, E_CHUNK), lambda s: (layer, 0, gated(s)[1])),
                  full, full, part, part],
        out_specs=tok,
        out_shape=jax.ShapeDtypeStruct((t, D_MODEL), F32),
        scratch_shapes=[pltpu.VMEM((E_CHUNK, tb), BF16), pltpu.VMEM((D_MODEL, tb), F32)],
        compiler_params=_params("arbitrary"),
        name="peer_dense",
    )(x, xn, u, v_t, r2, e2, l, cc)


def _peer(x, g, wk, u, v_t, layer, tb_sel, tb_dense):
    xn, r2, l, cc, e2 = _peer_select(x, g, wk, tb_sel)
    return _peer_dense(x, xn, u, v_t, layer, r2, e2, l, cc, tb_dense)


def _head_norm(z, head_mean, gn):
    zz = z * z
    hi = zz.astype(BF16)
    lo = (zz - hi.astype(F32)).astype(BF16)
    ms = (jnp.dot(hi, head_mean, preferred_element_type=F32)
          + jnp.dot(lo, head_mean, preferred_element_type=F32))
    return z * lax.rsqrt(ms + EPS) * gn


def _qkv_kernel(x_ref, gq_ref, gkv_ref, wq_ref, wkv_ref, hm_ref, qn_ref, kn_ref, q_ref, k_ref, v_ref):
    x = x_ref[...]
    hm = hm_ref[...]
    q = jnp.dot(_rms(x, gq_ref[...]).astype(BF16), wq_ref[...], preferred_element_type=F32)
    kv = jnp.dot(_rms(x, gkv_ref[...]).astype(BF16), wkv_ref[...], preferred_element_type=F32)
    q_ref[...] = _head_norm(q, hm, qn_ref[...])
    k_ref[...] = _head_norm(kv[:, :D_MODEL], hm, kn_ref[...])
    v_ref[...] = kv[:, D_MODEL:]


def _qkv(x, gq, gkv, wq, wkv, qn, kn, tb):
    t = x.shape[0]
    head_id = np.arange(D_MODEL) // HEAD_DIM
    head_mean = jnp.asarray((head_id[:, None] == head_id[None, :]) / HEAD_DIM, BF16)
    tok = pl.BlockSpec((tb, D_MODEL), lambda i: (i, 0))
    vec = pl.BlockSpec((1, D_MODEL), lambda i: (0, 0))
    sq = pl.BlockSpec((D_MODEL, D_MODEL), lambda i: (0, 0))
    out = jax.ShapeDtypeStruct((t, D_MODEL), F32)
    return pl.pallas_call(
        _qkv_kernel,
        grid=(t // tb,),
        in_specs=[tok, vec, vec, sq, pl.BlockSpec((D_MODEL, 2 * D_MODEL), lambda i: (0, 0)), sq, vec, vec],
        out_specs=[tok, tok, tok],
        out_shape=[out, out, out],
        compiler_params=_params("arbitrary"),
        name="qkv_proj",
    )(x, gq.reshape(1, D_MODEL), gkv.reshape(1, D_MODEL), wq, wkv, head_mean,
      jnp.tile(qn, N_HEADS).reshape(1, D_MODEL), jnp.tile(kn, N_HEADS).reshape(1, D_MODEL))


def _attend_rows(q_rows, k_band, v_band, bias_ref, valid):
    r = q_rows.shape[0]
    lane = lax.broadcasted_iota(jnp.int32, (1, 2 * HEAD_DIM), 1)
    first = lane < HEAD_DIM
    pairs = [slice(hp * 2 * HEAD_DIM, (hp + 1) * 2 * HEAD_DIM) for hp in range(N_HEADS // 2)]

    def scores(hp):
        q2 = q_rows[:, pairs[hp]]
        qm = jnp.concatenate([jnp.where(first, q2, 0.0), jnp.where(first, 0.0, q2)], axis=0).astype(BF16)
        s = lax.dot_general(qm, k_band[:, pairs[hp]], NT_DIMS, preferred_element_type=F32)
        s = s * (HEAD_DIM ** -0.5) + bias_ref[hp]
        return s if valid is None else jnp.where(valid, s, MASKED)

    def attend(hp, s):
        p = jnp.exp(s - jnp.max(s, axis=-1, keepdims=True))
        den = jnp.sum(p, axis=-1, keepdims=True)
        o2 = jnp.dot(p.astype(BF16), v_band[:, pairs[hp]], preferred_element_type=F32) / den
        return jnp.where(first, o2[:r], o2[r:])

    outs = []
    s_next = scores(0)
    for hp in range(len(pairs)):
        s_cur = s_next
        if hp + 1 < len(pairs):
            s_next = scores(hp + 1)
        outs.append(attend(hp, s_cur))
    return jnp.concatenate(outs, axis=-1)


def _attn_prompt_kernel(qb, q_ref, kp_ref, kc_ref, vp_ref, vc_ref, bias_ref, wo_ref, x_ref, o_ref,
                        kcat, vcat, oscr):
    i = pl.program_id(0)
    kcat[:qb] = kp_ref[...].astype(BF16)
    kcat[qb:] = kc_ref[...].astype(BF16)
    vcat[:qb] = vp_ref[...].astype(BF16)
    vcat[qb:] = vc_ref[...].astype(BF16)
    col = lax.broadcasted_iota(jnp.int32, (1, BAND), 1)

    def chunk(j, carry):
        r0 = pl.multiple_of(j * CHUNK, CHUNK)
        k0 = pl.multiple_of(qb - BAND_PAST + j * CHUNK, CHUNK)
        valid = (i * qb + j * CHUNK - BAND_PAST + col) >= 0
        o = _attend_rows(q_ref[pl.ds(r0, CHUNK), :], kcat[pl.ds(k0, BAND), :], vcat[pl.ds(k0, BAND), :],
                         bias_ref, valid)
        oscr[pl.ds(r0, CHUNK), :] = o.astype(BF16)
        return carry

    lax.fori_loop(0, qb // CHUNK, chunk, 0)
    o_ref[...] = x_ref[...] + jnp.dot(oscr[...], wo_ref[...], preferred_element_type=F32)


def _attn_prompt(x, q, k, v, bias, wo, qb):
    t = x.shape[0]
    cur = pl.BlockSpec((qb, D_MODEL), lambda i: (i, 0))
    prev = pl.BlockSpec((qb, D_MODEL), lambda i: (jnp.maximum(i - 1, 0), 0))
    return pl.pallas_call(
        functools.partial(_attn_prompt_kernel, qb),
        grid=(t // qb,),
        in_specs=[cur, prev, cur, prev, cur,
                  pl.BlockSpec((N_HEADS // 2, 2 * CHUNK, BAND), lambda i: (0, 0, 0)),
                  pl.BlockSpec((D_MODEL, D_MODEL), lambda i: (0, 0)),
                  cur],
        out_specs=cur,
        out_shape=jax.ShapeDtypeStruct((t, D_MODEL), F32),
        scratch_shapes=[pltpu.VMEM((2 * qb, D_MODEL), BF16), pltpu.VMEM((2 * qb, D_MODEL), BF16),
                        pltpu.VMEM((qb, D_MODEL), BF16)],
        compiler_params=_params("arbitrary"),
        name="attn_prompt",
    )(q, k, k, v, v, bias, wo, x)


def _attn_sample_kernel(rows, q_ref, ck_ref, kn_ref, cv_ref, vn_ref, bias_ref, wo_ref, x_ref, o_ref,
                        kcat, vcat):
    kcat[:rows] = ck_ref[0].astype(BF16)
    kcat[rows:] = kn_ref[0].astype(BF16)
    vcat[:rows] = cv_ref[0].astype(BF16)
    vcat[rows:] = vn_ref[0].astype(BF16)
    o = _attend_rows(q_ref[0], kcat[...], vcat[...], bias_ref, None)
    o_ref[0] = x_ref[0] + jnp.dot(o.astype(BF16), wo_ref[...], preferred_element_type=F32)


def _attn_sample(x, q, k, v, cache_k, cache_v, bias, wo):
    bsz, t, _ = x.shape
    rows = cache_k.shape[1]
    new = pl.BlockSpec((1, t, D_MODEL), lambda b: (b, 0, 0))
    old = pl.BlockSpec((1, rows, D_MODEL), lambda b: (b, 0, 0))
    return pl.pallas_call(
        functools.partial(_attn_sample_kernel, rows),
        grid=(bsz,),
        in_specs=[new, old, new, old, new,
                  pl.BlockSpec((N_HEADS // 2, 2 * t, rows + t), lambda b: (0, 0, 0)),
                  pl.BlockSpec((D_MODEL, D_MODEL), lambda b: (0, 0)),
                  new],
        out_specs=new,
        out_shape=jax.ShapeDtypeStruct((bsz, t, D_MODEL), F32),
        scratch_shapes=[pltpu.VMEM((rows + t, D_MODEL), BF16), pltpu.VMEM((rows + t, D_MODEL), BF16)],
        compiler_params=_params("arbitrary"),
        name="attn_sample",
    )(q, cache_k, k, cache_v, v, bias, wo, x)


def _band_bias(rel_bias, n_q, n_k):
    n_diag = n_q - 1 + n_k
    m = np.arange(n_diag)
    diag = rel_bias[:, np.clip(BAND_PAST + n_q - 1 - m, -REL_CLIP, REL_CLIP) + REL_CLIP]
    wrapped = jnp.tile(diag, (1, n_q + 1))[:, :n_q * (n_diag + 1)].reshape(N_HEADS, n_q, n_diag + 1)
    bias = wrapped[:, ::-1, :n_k]
    return bias.reshape(N_HEADS // 2, 2 * n_q, n_k)


def _trunk(x, hist, cache_k, cache_v, pos0, p, tt, tb_sel, tb_dense, tb_qkv, qb):
    bsz, t, _ = x.shape
    n = bsz * t
    x1, st = _pool_mixer(x, hist, pos0, p["norm_mix"][0], p["pool_w"][0], p["pool_scale"][0], tt)
    x1 = x1.reshape(n, D_MODEL)
    x2 = _peer(x1, p["norm_ffn"][0], p["wk"][0], p["u"], p["v_t"], 0,
               tb_sel, tb_dense)
    q, k, v = _qkv(x2, p["norm_mix"][1], p["kv_norm"], p["w_q"], p["w_kv"], p["q_norm"], p["k_norm"], tb_qkv)
    if cache_k is None:
        assert bsz == 1 and t % qb == 0 and qb >= BAND_PAST and pos0 == 0
        x3 = _attn_prompt(x2, q, k, v, _band_bias(p["rel_bias"], CHUNK, BAND), p["w_o"], qb)
    else:
        rows = cache_k.shape[1]
        assert pos0 % CHUNK == 0 and t <= CHUNK and rows == BAND_PAST and pos0 >= rows
        bias = _band_bias(p["rel_bias"], t, rows + t)
        x3 = _attn_sample(x2.reshape(bsz, t, D_MODEL), q.reshape(bsz, t, D_MODEL),
                          k.reshape(bsz, t, D_MODEL), v.reshape(bsz, t, D_MODEL),
                          cache_k.reshape(bsz, rows, D_MODEL), cache_v.reshape(bsz, rows, D_MODEL),
                          bias, p["w_o"]).reshape(n, D_MODEL)
    x4 = _peer(x3, p["norm_ffn"][1], p["wk"][1], p["u"], p["v_t"], 1,
               tb_sel, tb_dense)
    return (x4.reshape(bsz, t, D_MODEL), st[None],
            k.reshape(bsz, t, N_HEADS, HEAD_DIM), v.reshape(bsz, t, N_HEADS, HEAD_DIM))


def _prepare(norm_mix, norm_ffn, pool_w, pool_scale, kv_norm, w_kv, k_norm, w_q, q_norm, rel_bias, w_o,
             peer_wq, peer_keys, peer_u, peer_v):
    assert w_q.shape[0] == 1 and pool_w.shape[0] == 1
    return dict(
        norm_mix=norm_mix, norm_ffn=norm_ffn, pool_w=pool_w, pool_scale=pool_scale, kv_norm=kv_norm,
        k_norm=k_norm, q_norm=q_norm[0], rel_bias=rel_bias[0],
        w_kv=w_kv.astype(BF16), w_q=w_q[0].astype(BF16), w_o=w_o[0].astype(BF16),
        wk=[_fold_keys(peer_keys[i], peer_wq[i]) for i in range(peer_wq.shape[0])],
        u=peer_u.astype(BF16), v_t=jnp.swapaxes(peer_v, 1, 2).astype(BF16))


def kernel(x_prompt, x_sample, state_pool, cache_k, cache_v, norm_mix, norm_ffn, pool_w, pool_scale, kv_norm, w_kv, k_norm, w_q, q_norm, rel_bias, w_o, peer_wq, peer_keys, peer_u, peer_v):
    p = _prepare(norm_mix, norm_ffn, pool_w, pool_scale, kv_norm, w_kv, k_norm, w_q, q_norm, rel_bias, w_o,
                 peer_wq, peer_keys, peer_u, peer_v)
    hist0 = jnp.zeros((x_prompt.shape[0], POOL_HIST, D_MODEL), x_prompt.dtype)
    y_p, pool_p, k_p, v_p = _trunk(x_prompt, hist0, None, None, 0, p,
                                   tt=512, tb_sel=256, tb_dense=512, tb_qkv=512, qb=512)
    past_len = 4096
    y_s, pool_s, k_s, v_s = _trunk(x_sample, state_pool[0], cache_k, cache_v, past_len, p,
                                   tt=x_sample.shape[1], tb_sel=128, tb_dense=128, tb_qkv=128, qb=None)
    keep = min(BAND_PAST, x_prompt.shape[1])
    return (y_p, y_s, pool_p, pool_s, k_p[:, -keep:], v_p[:, -keep:], k_s, v_s)
```
